```python
import jax, jax.numpy as jnp
from jax import lax
import numpy as np

D_MODEL = 2048
BATCH = 16
SEQ = 2048
DEPTH = 1
DEC_BATCH = 2
DEC_SEQ = 8192
PAST_LEN = 128

HEAD_DIM = 128
N_Q_HEADS = 8
N_KV_HEADS = 2
GQA_GROUP = N_Q_HEADS // N_KV_HEADS
D_ATTN = N_Q_HEADS * HEAD_DIM
D_KV = N_KV_HEADS * HEAD_DIM
WINDOW = 128
BLOCK = 128
D_CONV = D_MODEL // 4
CONV_WIDTH = 3
N_MEM = 256
N_X_HEADS = 4
D_XATTN = N_X_HEADS * HEAD_DIM
D_MIX = D_ATTN + D_CONV + D_XATTN
IN_SIZES = (D_ATTN, D_KV, D_KV, D_CONV, D_CONV, D_CONV, D_XATTN)
D_IN = sum(IN_SIZES)
N_BRANCH = 3
D_FF = 5504
N_BUCKETS = 32
MAX_DISTANCE = 128
LN_EPS = 1e-5
NEG_INF = -1e30
ALPHA = (2 * DEPTH) ** 0.25
BETA = (8 * DEPTH) ** -0.25

kernel_name = "hybrid_gated_window_conv_memory_encoder"


def _band_geometry():
    t = np.arange(BLOCK)[:, None]
    j = np.arange(3 * BLOCK)[None, :]
    rel = j - BLOCK - t
    half = N_BUCKETS // 2
    max_exact = half // 2
    n = np.abs(rel)
    large = max_exact + (np.log(np.maximum(n, 1) / max_exact) / np.log(MAX_DISTANCE / max_exact) * (half - max_exact)).astype(np.int32)
    large = np.minimum(large, half - 1)
    bucket = (rel > 0).astype(np.int32) * half + np.where(n < max_exact, n, large)
    return rel, bucket.astype(np.int32)


def _layer_norm(x, g, b):
    xf = x.astype(jnp.float32)
    mu = xf.mean(-1, keepdims=True)
    var = jnp.square(xf - mu).mean(-1, keepdims=True)
    return ((xf - mu) * lax.rsqrt(var + LN_EPS) * g.astype(jnp.float32) + b.astype(jnp.float32)).astype(x.dtype)


def _swiglu(x, w_gate_up, w_down):
    g, u = jnp.split(x @ w_gate_up, 2, axis=-1)
    return (jax.nn.silu(g) * u) @ w_down


def _window_attention(q, k, v, rel_bias, sink):
    B, S = q.shape[:2]
    n = S // BLOCK
    qb = q.reshape(B, n, BLOCK, N_KV_HEADS, GQA_GROUP, HEAD_DIM)

    def band(t):
        tp = jnp.pad(t, ((0, 0), (BLOCK, BLOCK), (0, 0), (0, 0))).reshape(B, n + 2, BLOCK, N_KV_HEADS, HEAD_DIM)
        return jnp.concatenate([tp[:, :-2], tp[:, 1:-1], tp[:, 2:]], axis=2)

    kb, vb = band(k), band(v)
    s = jnp.einsum('bnqhgd,bnkhd->bnhgqk', qb, kb, preferred_element_type=jnp.float32) * (HEAD_DIM ** -0.5)
    s = s + rel_bias.reshape(1, 1, N_KV_HEADS, GQA_GROUP, BLOCK, 3 * BLOCK)
    rel, _ = _band_geometry()
    kpos = (np.arange(n)[:, None] - 1) * BLOCK + np.arange(3 * BLOCK)[None, :]
    valid = (np.abs(rel) <= WINDOW)[None] & ((kpos >= 0) & (kpos < S))[:, None, :]
    s = jnp.where(valid[None, :, None, None], s, NEG_INF)
    sink_l = sink.astype(jnp.float32).reshape(1, 1, N_KV_HEADS, GQA_GROUP, 1, 1)
    m = jnp.maximum(s.max(-1, keepdims=True), sink_l)
    p = jnp.exp(s - m)
    p = (p / (p.sum(-1, keepdims=True) + jnp.exp(sink_l - m))).astype(v.dtype)
    o = jnp.einsum('bnhgqk,bnkhd->bnqhgd', p, vb)
    return o.reshape(B, S, D_ATTN)


def _short_conv(u, w):
    up = jnp.pad(u, ((0, 0), (1, 1), (0, 0)))
    return up[:, :-2] * w[0] + up[:, 1:-1] * w[1] + up[:, 2:] * w[2]


def _memory_attention(q, mem, w_mem_kv):
    B, S = q.shape[:2]
    kv = (mem @ w_mem_kv).reshape(B, N_MEM, 2, N_X_HEADS, HEAD_DIM)
    s = jnp.einsum('bshd,bmhd->bhsm', q, kv[:, :, 0], preferred_element_type=jnp.float32) * (HEAD_DIM ** -0.5)
    p = jax.nn.softmax(s, axis=-1).astype(q.dtype)
    return jnp.einsum('bhsm,bmhd->bshd', p, kv[:, :, 1]).reshape(B, S, D_XATTN)


def _layer(x, mem, rel_bias, ln1_g, ln1_b, ffn1_w_gate_up, ffn1_w_down, w_in, conv_w, w_mem_kv,
           attn_sink, w_gate, b_gate, w_branch, w_o, ln2_g, ln2_b, ffn2_w_gate_up, ffn2_w_down, ln3_g, ln3_b):
    B, S, _ = x.shape
    x = _layer_norm(ALPHA * x + 0.5 * _swiglu(x, ffn1_w_gate_up, ffn1_w_down), ln1_g, ln1_b)
    cuts = list(np.cumsum(IN_SIZES)[:-1])
    q, k, v, cb, cc, ch, qx = jnp.split(x @ w_in, cuts, axis=-1)
    y_attn = _window_attention(q.reshape(B, S, N_Q_HEADS, HEAD_DIM), k.reshape(B, S, N_KV_HEADS, HEAD_DIM),
                               v.reshape(B, S, N_KV_HEADS, HEAD_DIM), rel_bias, attn_sink)
    y_conv = cb * _short_conv(cc * ch, conv_w)
    y_mem = _memory_attention(qx.reshape(B, S, N_X_HEADS, HEAD_DIM), mem, w_mem_kv)
    gates = jax.nn.sigmoid(x @ w_gate + b_gate).reshape(B, S, N_BRANCH, D_MODEL)
    merged = (gates[:, :, 0] * (y_attn @ w_branch[:D_ATTN])
              + gates[:, :, 1] * (y_conv @ w_branch[D_ATTN:D_ATTN + D_CONV])
              + gates[:, :, 2] * (y_mem @ w_branch[D_ATTN + D_CONV:]))
    x = _layer_norm(ALPHA * x + merged @ w_o, ln2_g, ln2_b)
    x = _layer_norm(ALPHA * x + 0.5 * _swiglu(x, ffn2_w_gate_up, ffn2_w_down), ln3_g, ln3_b)
    return x


def _forward(x, mem, rel_bias, ln1_g, ln1_b, ffn1_w_gate_up, ffn1_w_down, w_in, conv_w, w_mem_kv,
             attn_sink, w_gate, b_gate, w_branch, w_o, ln2_g, ln2_b, ffn2_w_gate_up, ffn2_w_down, ln3_g, ln3_b):
    for l in range(DEPTH):
        x = _layer(x, mem, rel_bias, ln1_g[l], ln1_b[l], ffn1_w_gate_up[l], ffn1_w_down[l], w_in[l], conv_w[l],
                   w_mem_kv[l], attn_sink[l], w_gate[l], b_gate[l], w_branch[l], w_o[l], ln2_g[l], ln2_b[l],
                   ffn2_w_gate_up[l], ffn2_w_down[l], ln3_g[l], ln3_b[l])
    return x


def setup_inputs(seed: int = 0) -> dict:
    key = jax.random.key(seed)
    ks = jax.random.split(key, 24)
    f32 = jnp.float32
    nrm = lambda k, shape, s: jax.random.normal(k, shape, f32) * s
    L = DEPTH
    return {
        "x_prompt": nrm(ks[0], (BATCH, SEQ, D_MODEL), 1.0),
        "x_sample": nrm(ks[1], (DEC_BATCH, DEC_SEQ, D_MODEL), 1.0),
        "mem_prompt": nrm(ks[2], (BATCH, N_MEM, D_MODEL), 1.0),
        "mem_sample": nrm(ks[3], (DEC_BATCH, N_MEM, D_MODEL), 1.0),
        "rel_bias_table": nrm(ks[4], (N_BUCKETS, N_Q_HEADS), 0.5),
        "ln1_g": 1.0 + nrm(ks[5], (L, D_MODEL), 0.02),
        "ln1_b": nrm(ks[6], (L, D_MODEL), 0.02),
        "ffn1_w_gate_up": nrm(ks[7], (L, D_MODEL, 2 * D_FF), D_MODEL ** -0.5),
        "ffn1_w_down": nrm(ks[8], (L, D_FF, D_MODEL), BETA * D_FF ** -0.5),
        "w_in": nrm(ks[9], (L, D_MODEL, D_IN), D_MODEL ** -0.5),
        "conv_w": nrm(ks[10], (L, CONV_WIDTH, D_CONV), CONV_WIDTH ** -0.5),
        "w_mem_kv": nrm(ks[11], (L, D_MODEL, 2 * D_XATTN), D_MODEL ** -0.5),
        "attn_sink": nrm(ks[12], (L, N_Q_HEADS), 0.5),
        "w_gate": nrm(ks[13], (L, D_MODEL, N_BRANCH * D_MODEL), D_MODEL ** -0.5),
        "b_gate": nrm(ks[14], (L, N_BRANCH * D_MODEL), 0.1),
        "w_branch": nrm(ks[15], (L, D_MIX, D_MODEL), BETA * D_MIX ** -0.5),
        "w_o": nrm(ks[16], (L, D_MODEL, D_MODEL), BETA * D_MODEL ** -0.5),
        "ln2_g": 1.0 + nrm(ks[17], (L, D_MODEL), 0.02),
        "ln2_b": nrm(ks[18], (L, D_MODEL), 0.02),
        "ffn2_w_gate_up": nrm(ks[19], (L, D_MODEL, 2 * D_FF), D_MODEL ** -0.5),
        "ffn2_w_down": nrm(ks[20], (L, D_FF, D_MODEL), BETA * D_FF ** -0.5),
        "ln3_g": 1.0 + nrm(ks[21], (L, D_MODEL), 0.02),
        "ln3_b": nrm(ks[22], (L, D_MODEL), 0.02),
    }


def reference(x_prompt, x_sample, mem_prompt, mem_sample, rel_bias_table, ln1_g, ln1_b, ffn1_w_gate_up,
              ffn1_w_down, w_in, conv_w, w_mem_kv, attn_sink, w_gate, b_gate, w_branch, w_o, ln2_g, ln2_b,
              ffn2_w_gate_up, ffn2_w_down, ln3_g, ln3_b):
    _, bucket = _band_geometry()
    rel_bias = jnp.transpose(rel_bias_table[bucket], (2, 0, 1)).astype(jnp.float32)
    weights = (ln1_g, ln1_b, ffn1_w_gate_up, ffn1_w_down, w_in, conv_w, w_mem_kv, attn_sink, w_gate, b_gate,
               w_branch, w_o, ln2_g, ln2_b, ffn2_w_gate_up, ffn2_w_down, ln3_g, ln3_b)
    y_prompt = _forward(x_prompt, mem_prompt, rel_bias, *weights)
    y_sample = _forward(x_sample, mem_sample, rel_bias, *weights)
    return (y_prompt, y_sample)
```

```python
import functools

import jax
import jax.numpy as jnp
import numpy as np
from jax import lax
from jax.experimental import pallas as pl
from jax.experimental.pallas import tpu as pltpu

D_MODEL = 2048
DEPTH = 1
HEAD_DIM = 128
N_Q_HEADS = 8
N_KV_HEADS = 2
GQA_GROUP = N_Q_HEADS // N_KV_HEADS
D_ATTN = N_Q_HEADS * HEAD_DIM
D_KV = N_KV_HEADS * HEAD_DIM
WINDOW = 128
BLOCK = 128
D_CONV = D_MODEL // 4
CONV_WIDTH = 3
N_MEM = 256
N_X_HEADS = 4
D_XATTN = N_X_HEADS * HEAD_DIM
D_MIX = D_ATTN + D_CONV + D_XATTN
D_QKV = D_ATTN + 2 * D_KV
D_IN = D_QKV + 3 * D_CONV + D_XATTN
N_BRANCH = 3
D_FF = 5504
N_BUCKETS = 32
MAX_DISTANCE = 128
LN_EPS = 1e-5
NEG_INF = -1e30
ALPHA = (2 * DEPTH) ** 0.25
ATTN_SCALE = HEAD_DIM ** -0.5

VMEM_LIMIT_BYTES = 56 * 1024 * 1024
SUBLANES = 8
FF_TILE = 512
D_FF_PAD = -(-D_FF // FF_TILE) * FF_TILE

F32 = jnp.float32
BF16 = jnp.bfloat16


def _pick_tile(n, target):
    t = min(n, target)
    while n % t:
        t //= 2
    return t


def _params(*semantics):
    return pltpu.CompilerParams(dimension_semantics=semantics, vmem_limit_bytes=VMEM_LIMIT_BYTES)


def _resident(shape):
    zeros = (0,) * len(shape)
    return pl.BlockSpec(shape, lambda *_: zeros, pipeline_mode=pl.Buffered(1))


def _layer_norm(y, g, b):
    mu = jnp.mean(y, axis=-1, keepdims=True)
    yc = y - mu
    var = jnp.mean(yc * yc, axis=-1, keepdims=True)
    return yc * lax.rsqrt(var + LN_EPS) * g + b


def _ffn_ln_kernel(x_ref, wg_ref, wu_ref, wd_ref, g_ref, b_ref, *refs, emit_bf16):
    if emit_bf16:
        o_ref, ob_ref, xb_ref = refs
    else:
        o_ref, xb_ref = refs
    k = pl.program_id(1)

    @pl.when(k == 0)
    def _():
        xb_ref[...] = x_ref[...].astype(BF16)
        o_ref[...] = jnp.zeros_like(o_ref)

    xb = xb_ref[...]
    hg = jnp.dot(xb, wg_ref[...], preferred_element_type=F32)
    hu = jnp.dot(xb, wu_ref[...], preferred_element_type=F32)
    act = (hg * jax.nn.sigmoid(hg) * hu).astype(BF16)
    o_ref[...] += jnp.dot(act, wd_ref[...], preferred_element_type=F32)

    @pl.when(k == pl.num_programs(1) - 1)
    def _():
        y = ALPHA * x_ref[...] + 0.5 * o_ref[...]
        out = _layer_norm(y, g_ref[...], b_ref[...])
        o_ref[...] = out
        if emit_bf16:
            ob_ref[...] = out.astype(BF16)


def _ffn_ln(x, wg, wu, wd, g, b, *, emit_bf16, tm_target=512):
    m = x.shape[0]
    tm = _pick_tile(m, tm_target)
    nk = D_FF_PAD // FF_TILE
    row = pl.BlockSpec((tm, D_MODEL), lambda i, k: (i, 0))
    vec = pl.BlockSpec((1, D_MODEL), lambda i, k: (0, 0))
    out_shape = [jax.ShapeDtypeStruct((m, D_MODEL), F32)]
    out_specs = [row]
    if emit_bf16:
        out_shape.append(jax.ShapeDtypeStruct((m, D_MODEL), BF16))
        out_specs.append(row)
    return pl.pallas_call(
        functools.partial(_ffn_ln_kernel, emit_bf16=emit_bf16),
        grid=(m // tm, nk),
        in_specs=[
            row,
            pl.BlockSpec((D_MODEL, FF_TILE), lambda i, k: (0, k)),
            pl.BlockSpec((D_MODEL, FF_TILE), lambda i, k: (0, k)),
            pl.BlockSpec((FF_TILE, D_MODEL), lambda i, k: (k, 0)),
            vec,
            vec,
        ],
        out_specs=out_specs,
        out_shape=out_shape,
        scratch_shapes=[pltpu.VMEM((tm, D_MODEL), BF16)],
        compiler_params=_params("parallel", "arbitrary"),
        name="ffn_ln",
    )(x, wg, wu, wd, g, b)


def _matmul_kernel(x_ref, w_ref, *refs, sigmoid):
    if sigmoid:
        b_ref, o_ref = refs
    else:
        (o_ref,) = refs
    acc = jnp.dot(x_ref[...], w_ref[...], preferred_element_type=F32)
    if sigmoid:
        acc = jax.nn.sigmoid(acc + b_ref[...])
    o_ref[...] = acc.astype(o_ref.dtype)


def _matmul(x, w, bias=None, *, out_dtype, tm_target=1024, tn_target=1024):
    m, kdim = x.shape
    n = w.shape[1]
    tm = _pick_tile(m, tm_target)
    tn = _pick_tile(n, tn_target)
    in_specs = [
        pl.BlockSpec((tm, kdim), lambda i, j: (i, 0)),
        pl.BlockSpec((kdim, tn), lambda i, j: (0, j)),
    ]
    args = [x, w]
    if bias is not None:
        in_specs.append(pl.BlockSpec((1, tn), lambda i, j: (0, j)))
        args.append(bias)
    return pl.pallas_call(
        functools.partial(_matmul_kernel, sigmoid=bias is not None),
        grid=(m // tm, n // tn),
        in_specs=in_specs,
        out_specs=pl.BlockSpec((tm, tn), lambda i, j: (i, j)),
        out_shape=jax.ShapeDtypeStruct((m, n), out_dtype),
        compiler_params=_params("parallel", "arbitrary"),
        name="matmul_sigmoid" if bias is not None else "matmul",
    )(*args)


def _in_proj_kernel(x_ref, w_ref, qkv_ref, cb_ref, u_ref, qx_ref):
    x = x_ref[...]

    def proj(lo, width):
        return jnp.dot(x, w_ref[:, lo:lo + width], preferred_element_type=F32)

    qkv_ref[...] = proj(0, D_QKV).astype(BF16)
    cb_ref[...] = proj(D_QKV, D_CONV)
    u_ref[...] = proj(D_QKV + D_CONV, D_CONV) * proj(D_QKV + 2 * D_CONV, D_CONV)
    qx_ref[...] = proj(D_QKV + 3 * D_CONV, D_XATTN).astype(BF16)


def _in_proj(xb, w_in, *, tm_target=512):
    m = xb.shape[0]
    tm = _pick_tile(m, tm_target)

    def row(width):
        return pl.BlockSpec((tm, width), lambda i: (i, 0))

    return pl.pallas_call(
        _in_proj_kernel,
        grid=(m // tm,),
        in_specs=[row(D_MODEL), _resident((D_MODEL, D_IN))],
        out_specs=[row(D_QKV), row(D_CONV), row(D_CONV), row(D_XATTN)],
        out_shape=[
            jax.ShapeDtypeStruct((m, D_QKV), BF16),
            jax.ShapeDtypeStruct((m, D_CONV), F32),
            jax.ShapeDtypeStruct((m, D_CONV), F32),
            jax.ShapeDtypeStruct((m, D_XATTN), BF16),
        ],
        compiler_params=_params("parallel"),
        name="in_proj",
    )(xb, w_in)


def _dot_nt(a, b):
    return lax.dot_general(a, b, (((1,), (1,)), ((), ())), preferred_element_type=F32)


def _mixers_kernel(sink_ref, q_ref, kp_ref, kc_ref, kn_ref, vp_ref, vc_ref, vn_ref, bias_ref,
                   cb_ref, up_ref, uc_ref, un_ref, cw_ref, qx_ref, kvm_ref, o_ref, *, tq, n_blocks):
    i = pl.program_id(1)
    last_i = pl.num_programs(1) - 1
    r_blocks = tq // BLOCK

    col = lax.broadcasted_iota(jnp.int32, (BLOCK, 3 * BLOCK), 1)
    rowi = lax.broadcasted_iota(jnp.int32, (BLOCK, 3 * BLOCK), 0)
    band = jnp.abs(col - BLOCK - rowi) <= WINDOW
    for r in range(r_blocks):
        blk = i * r_blocks + r
        lo = jnp.where(blk == 0, BLOCK, 0)
        hi = jnp.where(blk == n_blocks - 1, 2 * BLOCK, 3 * BLOCK)
        valid = band & (col >= lo) & (col < hi)

        def window(p_ref, c_ref, n_ref):
            parts = []
            parts.append(p_ref[0] if r == 0 else c_ref[0, (r - 1) * BLOCK:r * BLOCK, :])
            parts.append(c_ref[0, r * BLOCK:(r + 1) * BLOCK, :])
            parts.append(n_ref[0] if r == r_blocks - 1 else c_ref[0, (r + 1) * BLOCK:(r + 2) * BLOCK, :])
            return jnp.concatenate(parts, axis=0)

        k3 = window(kp_ref, kc_ref, kn_ref)
        v3 = window(vp_ref, vc_ref, vn_ref)
        rows = slice(r * BLOCK, (r + 1) * BLOCK)
        for g in range(N_KV_HEADS):
            kg = k3[:, g * HEAD_DIM:(g + 1) * HEAD_DIM]
            vg = v3[:, g * HEAD_DIM:(g + 1) * HEAD_DIM]
            heads = [g * GQA_GROUP + j for j in range(GQA_GROUP)]
            q4 = jnp.concatenate(
                [q_ref[0, rows, h * HEAD_DIM:(h + 1) * HEAD_DIM] for h in heads], axis=0)
            s4 = _dot_nt(q4, kg) * ATTN_SCALE
            ps, denoms = [], []
            for j, h in enumerate(heads):
                s = s4[j * BLOCK:(j + 1) * BLOCK] + bias_ref[h]
                s = jnp.where(valid, s, NEG_INF)
                sink = sink_ref[h]
                m = jnp.maximum(jnp.max(s, axis=-1, keepdims=True), sink)
                p = jnp.exp(s - m)
                denoms.append(jnp.sum(p, axis=-1, keepdims=True) + jnp.exp(sink - m))
                ps.append(p.astype(BF16))
            o4 = jnp.dot(jnp.concatenate(ps, axis=0), vg, preferred_element_type=F32)
            for j, h in enumerate(heads):
                o = o4[j * BLOCK:(j + 1) * BLOCK] / denoms[j]
                o_ref[0, rows, h * HEAD_DIM:(h + 1) * HEAD_DIM] = o.astype(BF16)

    u = uc_ref[0]
    t = lax.broadcasted_iota(jnp.int32, (tq, D_CONV), 0)
    prev_row = jnp.where(i == 0, 0.0, up_ref[0, SUBLANES - 1:SUBLANES, :])
    next_row = jnp.where(i == last_i, 0.0, un_ref[0, 0:1, :])
    u_prev = jnp.where(t == 0, prev_row, pltpu.roll(u, 1, axis=0))
    u_next = jnp.where(t == tq - 1, next_row, pltpu.roll(u, tq - 1, axis=0))
    conv = u_prev * cw_ref[0:1, :] + u * cw_ref[1:2, :] + u_next * cw_ref[2:3, :]
    o_ref[0, :, D_ATTN:D_ATTN + D_CONV] = (cb_ref[0] * conv).astype(BF16)

    for h in range(N_X_HEADS):
        cols = slice(h * HEAD_DIM, (h + 1) * HEAD_DIM)
        km = kvm_ref[0, :, cols]
        vm = kvm_ref[0, :, D_XATTN + h * HEAD_DIM:D_XATTN + (h + 1) * HEAD_DIM]
        s = _dot_nt(qx_ref[0, :, cols], km) * ATTN_SCALE
        p = jnp.exp(s - jnp.max(s, axis=-1, keepdims=True))
        denom = jnp.sum(p, axis=-1, keepdims=True)
        o = jnp.dot(p.astype(BF16), vm, preferred_element_type=F32) / denom
        base = D_ATTN + D_CONV
        o_ref[0, :, base + h * HEAD_DIM:base + (h + 1) * HEAD_DIM] = o.astype(BF16)


def _mixers(qkv, cb, u, qx, kvm, rel_bias, sink, conv_w, *, tq_target=512):
    bsz, seq, _ = qkv.shape
    tq = _pick_tile(seq, tq_target)
    r_blocks = tq // BLOCK
    n_blocks = seq // BLOCK
    n_oct = seq // SUBLANES
    oct_per_tile = tq // SUBLANES
    k_col = D_ATTN // D_KV
    v_col = k_col + 1

    def main(width, colblk=0):
        return pl.BlockSpec((1, tq, width), lambda b, i: (b, i, colblk))

    def prev_blk(colblk):
        return pl.BlockSpec((1, BLOCK, D_KV), lambda b, i: (b, jnp.maximum(i * r_blocks - 1, 0), colblk))

    def next_blk(colblk):
        return pl.BlockSpec(
            (1, BLOCK, D_KV), lambda b, i: (b, jnp.minimum((i + 1) * r_blocks, n_blocks - 1), colblk))

    in_specs = [
        pl.BlockSpec(memory_space=pltpu.SMEM),
        main(D_ATTN),
        prev_blk(k_col), main(D_KV, k_col), next_blk(k_col),
        prev_blk(v_col), main(D_KV, v_col), next_blk(v_col),
        pl.BlockSpec((N_Q_HEADS, BLOCK, 3 * BLOCK), lambda b, i: (0, 0, 0)),
        main(D_CONV),
        pl.BlockSpec((1, SUBLANES, D_CONV), lambda b, i: (b, jnp.maximum(i * oct_per_tile - 1, 0), 0)),
        main(D_CONV),
        pl.BlockSpec((1, SUBLANES, D_CONV), lambda b, i: (b, jnp.minimum((i + 1) * oct_per_tile, n_oct - 1), 0)),
        pl.BlockSpec((CONV_WIDTH, D_CONV), lambda b, i: (0, 0)),
        main(D_XATTN),
        pl.BlockSpec((1, N_MEM, 2 * D_XATTN), lambda b, i: (b, 0, 0)),
    ]
    return pl.pallas_call(
        functools.partial(_mixers_kernel, tq=tq, n_blocks=n_blocks),
        grid=(bsz, seq // tq),
        in_specs=in_specs,
        out_specs=pl.BlockSpec((1, tq, D_MIX), lambda b, i: (b, i, 0)),
        out_shape=jax.ShapeDtypeStruct((bsz, seq, D_MIX), BF16),
        compiler_params=_params("parallel", "arbitrary"),
        name="mixers",
    )(sink, qkv, qkv, qkv, qkv, qkv, qkv, qkv, rel_bias, cb, u, u, u, conv_w, qx, kvm)


def _merge_ln_kernel(x_ref, y_ref, gate_ref, wb_ref, wo_ref, g_ref, b_ref, o_ref):
    bounds = (0, D_ATTN, D_ATTN + D_CONV, D_MIX)
    merged = None
    for br in range(N_BRANCH):
        lo, hi = bounds[br], bounds[br + 1]
        proj = jnp.dot(y_ref[:, lo:hi], wb_ref[lo:hi, :], preferred_element_type=F32)
        term = gate_ref[:, br * D_MODEL:(br + 1) * D_MODEL] * proj
        merged = term if merged is None else merged + term
    mixed = jnp.dot(merged.astype(BF16), wo_ref[...], preferred_element_type=F32)
    o_ref[...] = _layer_norm(ALPHA * x_ref[...] + mixed, g_ref[...], b_ref[...])


def _merge_ln(x1, ymix, gates, wb, wo, g, b, *, tm_target=256):
    m = x1.shape[0]
    tm = _pick_tile(m, tm_target)

    def row(width):
        return pl.BlockSpec((tm, width), lambda i: (i, 0))

    return pl.pallas_call(
        _merge_ln_kernel,
        grid=(m // tm,),
        in_specs=[
            row(D_MODEL), row(D_MIX), row(N_BRANCH * D_MODEL),
            _resident((D_MIX, D_MODEL)), _resident((D_MODEL, D_MODEL)),
            _resident((1, D_MODEL)), _resident((1, D_MODEL)),
        ],
        out_specs=row(D_MODEL),
        out_shape=jax.ShapeDtypeStruct((m, D_MODEL), F32),
        compiler_params=_params("parallel"),
        name="merge_ln",
    )(x1, ymix, gates, wb, wo, g, b)


def _bucket_table():
    t = np.arange(BLOCK)[:, None]
    j = np.arange(3 * BLOCK)[None, :]
    rel = j - BLOCK - t
    half = N_BUCKETS // 2
    max_exact = half // 2
    n = np.abs(rel)
    large = max_exact + (np.log(np.maximum(n, 1) / max_exact) / np.log(MAX_DISTANCE / max_exact)
                         * (half - max_exact)).astype(np.int32)
    large = np.minimum(large, half - 1)
    bucket = (rel > 0).astype(np.int32) * half + np.where(n < max_exact, n, large)
    return bucket.astype(np.int32)


def _prep_layer(l, ln1_g, ln1_b, ffn1_w_gate_up, ffn1_w_down, w_in, conv_w, w_mem_kv, attn_sink, w_gate,
                b_gate, w_branch, w_o, ln2_g, ln2_b, ffn2_w_gate_up, ffn2_w_down, ln3_g, ln3_b):
    pad = D_FF_PAD - D_FF

    def ffn(w_gate_up, w_down):
        wg = jnp.pad(w_gate_up[l, :, :D_FF].astype(BF16), ((0, 0), (0, pad)))
        wu = jnp.pad(w_gate_up[l, :, D_FF:].astype(BF16), ((0, 0), (0, pad)))
        wd = jnp.pad(w_down[l].astype(BF16), ((0, pad), (0, 0)))
        return wg, wu, wd

    vec = lambda a: a[l].reshape(1, -1).astype(F32)
    return dict(
        ffn1=ffn(ffn1_w_gate_up, ffn1_w_down), ln1=(vec(ln1_g), vec(ln1_b)),
        w_in=w_in[l].astype(BF16), conv_w=conv_w[l].astype(F32), w_mem_kv=w_mem_kv[l].astype(BF16),
        sink=attn_sink[l].astype(F32), w_gate=w_gate[l].astype(BF16), b_gate=vec(b_gate),
        w_branch=w_branch[l].astype(BF16), w_o=w_o[l].astype(BF16), ln2=(vec(ln2_g), vec(ln2_b)),
        ffn2=ffn(ffn2_w_gate_up, ffn2_w_down), ln3=(vec(ln3_g), vec(ln3_b)),
    )


def _layer(x, mem_b, rel_bias, p, bsz, seq):
    x1, x1b = _ffn_ln(x, *p["ffn1"], *p["ln1"], emit_bf16=True)
    qkv, cb, u, qx = _in_proj(x1b, p["w_in"])
    gates = _matmul(x1b, p["w_gate"], p["b_gate"], out_dtype=F32)
    kvm = _matmul(mem_b, p["w_mem_kv"], out_dtype=BF16)
    ymix = _mixers(
        qkv.reshape(bsz, seq, D_QKV), cb.reshape(bsz, seq, D_CONV), u.reshape(bsz, seq, D_CONV),
        qx.reshape(bsz, seq, D_XATTN), kvm.reshape(bsz, N_MEM, 2 * D_XATTN), rel_bias, p["sink"], p["conv_w"])
    x2 = _merge_ln(x1, ymix.reshape(bsz * seq, D_MIX), gates, p["w_branch"], p["w_o"], *p["ln2"])
    (x3,) = _ffn_ln(x2, *p["ffn2"], *p["ln3"], emit_bf16=False)
    return x3


def _forward(x, mem, rel_bias, layers):
    bsz, seq, _ = x.shape
    h = x.reshape(bsz * seq, D_MODEL)
    mem_b = mem.reshape(bsz * N_MEM, D_MODEL).astype(BF16)
    for p in layers:
        h = _layer(h, mem_b, rel_bias, p, bsz, seq)
    return h.reshape(bsz, seq, D_MODEL)


def kernel(x_prompt, x_sample, mem_prompt, mem_sample, rel_bias_table, ln1_g, ln1_b, ffn1_w_gate_up, ffn1_w_down, w_in, conv_w, w_mem_kv, attn_sink, w_gate, b_gate, w_branch, w_o, ln2_g, ln2_b, ffn2_w_gate_up, ffn2_w_down, ln3_g, ln3_b):
    weights = (ln1_g, ln1_b, ffn1_w_gate_up, ffn1_w_down, w_in, conv_w, w_mem_kv, attn_sink, w_gate, b_gate,
               w_branch, w_o, ln2_g, ln2_b, ffn2_w_gate_up, ffn2_w_down, ln3_g, ln3_b)
    layers = [_prep_layer(l, *weights) for l in range(ln1_g.shape[0])]
    rel_bias = jnp.transpose(rel_bias_table[_bucket_table()], (2, 0, 1)).astype(F32)
    y_prompt = _forward(x_prompt, mem_prompt, rel_bias, layers)
    y_sample = _forward(x_sample, mem_sample, rel_bias, layers)
    return (y_prompt, y_sample)
```

```python
import functools

import jax
import jax.numpy as jnp
import numpy as np
from jax import lax
from jax.experimental import pallas as pl
from jax.experimental.pallas import tpu as pltpu

D_MODEL = 2048
DEPTH = 1
HEAD_DIM = 128
N_Q_HEADS = 8
N_KV_HEADS = 2
GQA_GROUP = N_Q_HEADS // N_KV_HEADS
D_ATTN = N_Q_HEADS * HEAD_DIM
D_KV = N_KV_HEADS * HEAD_DIM
WINDOW = 128
BLOCK = 128
D_CONV = D_MODEL // 4
CONV_WIDTH = 3
N_MEM = 256
N_X_HEADS = 4
D_XATTN = N_X_HEADS * HEAD_DIM
D_MIX = D_ATTN + D_CONV + D_XATTN
D_QKV = D_ATTN + 2 * D_KV
D_IN = D_QKV + 3 * D_CONV + D_XATTN
N_BRANCH = 3
D_FF = 5504
N_BUCKETS = 32
MAX_DISTANCE = 128
LN_EPS = 1e-5
NEG_INF = -1e30
ALPHA = (2 * DEPTH) ** 0.25
ATTN_SCALE = HEAD_DIM ** -0.5

VMEM_LIMIT_BYTES = 56 * 1024 * 1024
SUBLANES = 8
FF_TILE = 512
D_FF_PAD = -(-D_FF // FF_TILE) * FF_TILE

F32 = jnp.float32
BF16 = jnp.bfloat16


def _pick_tile(n, target):
    t = min(n, target)
    while n % t:
        t //= 2
    return t


def _params(*semantics):
    return pltpu.CompilerParams(dimension_semantics=semantics, vmem_limit_bytes=VMEM_LIMIT_BYTES)


def _resident(shape):
    zeros = (0,) * len(shape)
    return pl.BlockSpec(shape, lambda *_: zeros, pipeline_mode=pl.Buffered(1))


def _layer_norm(y, g, b):
    mu = jnp.mean(y, axis=-1, keepdims=True)
    yc = y - mu
    var = jnp.mean(yc * yc, axis=-1, keepdims=True)
    return yc * lax.rsqrt(var + LN_EPS) * g + b


def _ffn_ln_kernel(x_hbm, wg_ref, wu_ref, wd_ref, g_ref, b_ref, o_ref, xbuf, xb_ref, sem, *, tm, rc):
    i = pl.program_id(0)
    k = pl.program_id(1)
    n_i = pl.num_programs(0)
    n_k = pl.num_programs(1)
    chunks = [pl.ds(c * rc, rc) for c in range(tm // rc)]

    def x_copy(tile):
        return pltpu.make_async_copy(x_hbm.at[pl.ds(tile * tm, tm), :], xbuf, sem)

    def swiglu(xb):
        hg = jnp.dot(xb, wg_ref[...], preferred_element_type=F32)
        hu = jnp.dot(xb, wu_ref[...], preferred_element_type=F32)
        return (hg * jax.nn.sigmoid(hg) * hu).astype(BF16)

    def down(act):
        return jnp.dot(act, wd_ref[...], preferred_element_type=F32)

    @pl.when((i == 0) & (k == 0))
    def _():
        x_copy(0).start()

    @pl.when(k == 0)
    def _():
        x_copy(i).wait()
        for rows in chunks:
            xc = xbuf[rows, :]
            xbc = xc.astype(BF16)
            xb_ref[rows, :] = xbc
            o_ref[rows, :] = (2.0 * ALPHA) * xc + down(swiglu(xbc))

    @pl.when((k == 1) & (i + 1 < n_i))
    def _():
        x_copy(i + 1).start()

    @pl.when((k > 0) & (k < n_k - 1))
    def _():
        o_ref[...] += down(swiglu(xb_ref[...]))

    @pl.when(k == n_k - 1)
    def _():
        act = swiglu(xb_ref[...])
        for c, rows in enumerate(chunks):
            acc = o_ref[rows, :] + down(act[c * rc:(c + 1) * rc])
            o_ref[rows, :] = _layer_norm(0.5 * acc, g_ref[...], b_ref[...])


def _ffn_ln(x, wg, wu, wd, g, b, *, tm_target=1024, rc_target=256):
    m = x.shape[0]
    tm = _pick_tile(m, tm_target)
    rc = _pick_tile(tm, rc_target)
    nk = D_FF_PAD // FF_TILE
    assert nk >= 3
    row = pl.BlockSpec((tm, D_MODEL), lambda i, k: (i, 0))
    vec = pl.BlockSpec((1, D_MODEL), lambda i, k: (0, 0))
    return pl.pallas_call(
        functools.partial(_ffn_ln_kernel, tm=tm, rc=rc),
        grid=(m // tm, nk),
        in_specs=[
            pl.BlockSpec(memory_space=pl.ANY),
            pl.BlockSpec((D_MODEL, FF_TILE), lambda i, k: (0, k)),
            pl.BlockSpec((D_MODEL, FF_TILE), lambda i, k: (0, k)),
            pl.BlockSpec((FF_TILE, D_MODEL), lambda i, k: (k, 0)),
            vec,
            vec,
        ],
        out_specs=row,
        out_shape=jax.ShapeDtypeStruct((m, D_MODEL), F32),
        scratch_shapes=[
            pltpu.VMEM((tm, D_MODEL), F32),
            pltpu.VMEM((tm, D_MODEL), BF16),
            pltpu.SemaphoreType.DMA(()),
        ],
        compiler_params=_params("arbitrary", "arbitrary"),
        name="ffn_ln",
    )(x, wg, wu, wd, g, b)


def _matmul_kernel(x_ref, w_ref, *refs, sigmoid):
    if sigmoid:
        b_ref, o_ref = refs
    else:
        (o_ref,) = refs
    acc = jnp.dot(x_ref[...].astype(BF16), w_ref[...], preferred_element_type=F32)
    if sigmoid:
        acc = jax.nn.sigmoid(acc + b_ref[...])
    o_ref[...] = acc.astype(o_ref.dtype)


def _matmul(x, w, bias=None, *, out_dtype, tm_target=1024, tn_target=1024):
    m, kdim = x.shape
    n = w.shape[1]
    tm = _pick_tile(m, tm_target)
    tn = _pick_tile(n, tn_target)
    in_specs = [
        pl.BlockSpec((tm, kdim), lambda i, j: (i, 0)),
        pl.BlockSpec((kdim, tn), lambda i, j: (0, j)),
    ]
    args = [x, w]
    if bias is not None:
        in_specs.append(pl.BlockSpec((1, tn), lambda i, j: (0, j)))
        args.append(bias)
    return pl.pallas_call(
        functools.partial(_matmul_kernel, sigmoid=bias is not None),
        grid=(m // tm, n // tn),
        in_specs=in_specs,
        out_specs=pl.BlockSpec((tm, tn), lambda i, j: (i, j)),
        out_shape=jax.ShapeDtypeStruct((m, n), out_dtype),
        compiler_params=_params("parallel", "arbitrary"),
        name="matmul_sigmoid" if bias is not None else "matmul",
    )(*args)


def _in_proj_kernel(x_ref, w_ref, qkv_ref, cb_ref, u_ref, qx_ref):
    x = x_ref[...].astype(BF16)

    def proj(lo, width):
        return jnp.dot(x, w_ref[:, lo:lo + width], preferred_element_type=F32)

    qkv_ref[...] = proj(0, D_QKV).astype(BF16)
    cb_ref[...] = proj(D_QKV, D_CONV)
    u_ref[...] = proj(D_QKV + D_CONV, D_CONV) * proj(D_QKV + 2 * D_CONV, D_CONV)
    qx_ref[...] = proj(D_QKV + 3 * D_CONV, D_XATTN).astype(BF16)


def _in_proj(x, w_in, *, tm_target=512):
    m = x.shape[0]
    tm = _pick_tile(m, tm_target)

    def row(width):
        return pl.BlockSpec((tm, width), lambda i: (i, 0))

    return pl.pallas_call(
        _in_proj_kernel,
        grid=(m // tm,),
        in_specs=[row(D_MODEL), _resident((D_MODEL, D_IN))],
        out_specs=[row(D_QKV), row(D_CONV), row(D_CONV), row(D_XATTN)],
        out_shape=[
            jax.ShapeDtypeStruct((m, D_QKV), BF16),
            jax.ShapeDtypeStruct((m, D_CONV), F32),
            jax.ShapeDtypeStruct((m, D_CONV), F32),
            jax.ShapeDtypeStruct((m, D_XATTN), BF16),
        ],
        compiler_params=_params("parallel"),
        name="in_proj",
    )(x, w_in)


def _dot_nt(a, b):
    return lax.dot_general(a, b, (((1,), (1,)), ((), ())), preferred_element_type=F32)


def _mixers_kernel(sink_ref, q_ref, kp_ref, kc_ref, kn_ref, vp_ref, vc_ref, vn_ref, bias_ref,
                   cb_ref, up_ref, uc_ref, un_ref, cw_ref, qx_ref, kvm_ref, o_ref, *, tq, n_blocks):
    i = pl.program_id(1)
    last_i = pl.num_programs(1) - 1
    r_blocks = tq // BLOCK

    col = lax.broadcasted_iota(jnp.int32, (BLOCK, 3 * BLOCK), 1)
    rowi = lax.broadcasted_iota(jnp.int32, (BLOCK, 3 * BLOCK), 0)
    band = jnp.abs(col - BLOCK - rowi) <= WINDOW
    for r in range(r_blocks):
        blk = i * r_blocks + r
        lo = jnp.where(blk == 0, BLOCK, 0)
        hi = jnp.where(blk == n_blocks - 1, 2 * BLOCK, 3 * BLOCK)
        valid = band & (col >= lo) & (col < hi)

        def window(p_ref, c_ref, n_ref):
            parts = []
            parts.append(p_ref[0] if r == 0 else c_ref[0, (r - 1) * BLOCK:r * BLOCK, :])
            parts.append(c_ref[0, r * BLOCK:(r + 1) * BLOCK, :])
            parts.append(n_ref[0] if r == r_blocks - 1 else c_ref[0, (r + 1) * BLOCK:(r + 2) * BLOCK, :])
            return jnp.concatenate(parts, axis=0)

        k3 = window(kp_ref, kc_ref, kn_ref)
        v3 = window(vp_ref, vc_ref, vn_ref)
        rows = slice(r * BLOCK, (r + 1) * BLOCK)
        for g in range(N_KV_HEADS):
            kg = k3[:, g * HEAD_DIM:(g + 1) * HEAD_DIM]
            vg = v3[:, g * HEAD_DIM:(g + 1) * HEAD_DIM]
            heads = [g * GQA_GROUP + j for j in range(GQA_GROUP)]
            q4 = jnp.concatenate(
                [q_ref[0, rows, h * HEAD_DIM:(h + 1) * HEAD_DIM] for h in heads], axis=0)
            s4 = _dot_nt(q4, kg) * ATTN_SCALE
            ps, denoms = [], []
            for j, h in enumerate(heads):
                s = s4[j * BLOCK:(j + 1) * BLOCK] + bias_ref[h]
                s = jnp.where(valid, s, NEG_INF)
                sink = sink_ref[h]
                m = jnp.maximum(jnp.max(s, axis=-1, keepdims=True), sink)
                p = jnp.exp(s - m)
                denoms.append(jnp.sum(p, axis=-1, keepdims=True) + jnp.exp(sink - m))
                ps.append(p.astype(BF16))
            o4 = jnp.dot(jnp.concatenate(ps, axis=0), vg, preferred_element_type=F32)
            for j, h in enumerate(heads):
                o = o4[j * BLOCK:(j + 1) * BLOCK] / denoms[j]
                o_ref[0, rows, h * HEAD_DIM:(h + 1) * HEAD_DIM] = o.astype(BF16)

    u = uc_ref[0]
    t = lax.broadcasted_iota(jnp.int32, (tq, D_CONV), 0)
    prev_row = jnp.where(i == 0, 0.0, up_ref[0, SUBLANES - 1:SUBLANES, :])
    next_row = jnp.where(i == last_i, 0.0, un_ref[0, 0:1, :])
    u_prev = jnp.where(t == 0, prev_row, pltpu.roll(u, 1, axis=0))
    u_next = jnp.where(t == tq - 1, next_row, pltpu.roll(u, tq - 1, axis=0))
    conv = u_prev * cw_ref[0:1, :] + u * cw_ref[1:2, :] + u_next * cw_ref[2:3, :]
    o_ref[0, :, D_ATTN:D_ATTN + D_CONV] = (cb_ref[0] * conv).astype(BF16)

    for h in range(N_X_HEADS):
        cols = slice(h * HEAD_DIM, (h + 1) * HEAD_DIM)
        km = kvm_ref[0, :, cols]
        vm = kvm_ref[0, :, D_XATTN + h * HEAD_DIM:D_XATTN + (h + 1) * HEAD_DIM]
        s = _dot_nt(qx_ref[0, :, cols], km) * ATTN_SCALE
        p = jnp.exp(s - jnp.max(s, axis=-1, keepdims=True))
        denom = jnp.sum(p, axis=-1, keepdims=True)
        o = jnp.dot(p.astype(BF16), vm, preferred_element_type=F32) / denom
        base = D_ATTN + D_CONV
        o_ref[0, :, base + h * HEAD_DIM:base + (h + 1) * HEAD_DIM] = o.astype(BF16)


def _mixers(qkv, cb, u, qx, kvm, rel_bias, sink, conv_w, *, tq_target=512):
    bsz, seq, _ = qkv.shape
    tq = _pick_tile(seq, tq_target)
    r_blocks = tq // BLOCK
    n_blocks = seq // BLOCK
    n_oct = seq // SUBLANES
    oct_per_tile = tq // SUBLANES
    k_col = D_ATTN // D_KV
    v_col = k_col + 1

    def main(width, colblk=0):
        return pl.BlockSpec((1, tq, width), lambda b, i: (b, i, colblk))

    def prev_blk(colblk):
        return pl.BlockSpec((1, BLOCK, D_KV), lambda b, i: (b, jnp.maximum(i * r_blocks - 1, 0), colblk))

    def next_blk(colblk):
        return pl.BlockSpec(
            (1, BLOCK, D_KV), lambda b, i: (b, jnp.minimum((i + 1) * r_blocks, n_blocks - 1), colblk))

    in_specs = [
        pl.BlockSpec(memory_space=pltpu.SMEM),
        main(D_ATTN),
        prev_blk(k_col), main(D_KV, k_col), next_blk(k_col),
        prev_blk(v_col), main(D_KV, v_col), next_blk(v_col),
        pl.BlockSpec((N_Q_HEADS, BLOCK, 3 * BLOCK), lambda b, i: (0, 0, 0)),
        main(D_CONV),
        pl.BlockSpec((1, SUBLANES, D_CONV), lambda b, i: (b, jnp.maximum(i * oct_per_tile - 1, 0), 0)),
        main(D_CONV),
        pl.BlockSpec((1, SUBLANES, D_CONV), lambda b, i: (b, jnp.minimum((i + 1) * oct_per_tile, n_oct - 1), 0)),
        pl.BlockSpec((CONV_WIDTH, D_CONV), lambda b, i: (0, 0)),
        main(D_XATTN),
        pl.BlockSpec((1, N_MEM, 2 * D_XATTN), lambda b, i: (b, 0, 0)),
    ]
    return pl.pallas_call(
        functools.partial(_mixers_kernel, tq=tq, n_blocks=n_blocks),
        grid=(bsz, seq // tq),
        in_specs=in_specs,
        out_specs=pl.BlockSpec((1, tq, D_MIX), lambda b, i: (b, i, 0)),
        out_shape=jax.ShapeDtypeStruct((bsz, seq, D_MIX), BF16),
        compiler_params=_params("parallel", "arbitrary"),
        name="mixers",
    )(sink, qkv, qkv, qkv, qkv, qkv, qkv, qkv, rel_bias, cb, u, u, u, conv_w, qx, kvm)


def _merge_ln_kernel(x_ref, y_ref, gate_ref, wb_ref, wo_ref, g_ref, b_ref, o_ref):
    bounds = (0, D_ATTN, D_ATTN + D_CONV, D_MIX)
    merged = None
    for br in range(N_BRANCH):
        lo, hi = bounds[br], bounds[br + 1]
        proj = jnp.dot(y_ref[:, lo:hi], wb_ref[lo:hi, :], preferred_element_type=F32)
        term = gate_ref[:, br * D_MODEL:(br + 1) * D_MODEL] * proj
        merged = term if merged is None else merged + term
    mixed = jnp.dot(merged.astype(BF16), wo_ref[...], preferred_element_type=F32)
    o_ref[...] = _layer_norm(ALPHA * x_ref[...] + mixed, g_ref[...], b_ref[...])


def _merge_ln(x1, ymix, gates, wb, wo, g, b, *, tm_target=256):
    m = x1.shape[0]
    tm = _pick_tile(m, tm_target)

    def row(width):
        return pl.BlockSpec((tm, width), lambda i: (i, 0))

    return pl.pallas_call(
        _merge_ln_kernel,
        grid=(m // tm,),
        in_specs=[
            row(D_MODEL), row(D_MIX), row(N_BRANCH * D_MODEL),
            _resident((D_MIX, D_MODEL)), _resident((D_MODEL, D_MODEL)),
            _resident((1, D_MODEL)), _resident((1, D_MODEL)),
        ],
        out_specs=row(D_MODEL),
        out_shape=jax.ShapeDtypeStruct((m, D_MODEL), F32),
        compiler_params=_params("parallel"),
        name="merge_ln",
    )(x1, ymix, gates, wb, wo, g, b)


def _bucket_table():
    t = np.arange(BLOCK)[:, None]
    j = np.arange(3 * BLOCK)[None, :]
    rel = j - BLOCK - t
    half = N_BUCKETS // 2
    max_exact = half // 2
    n = np.abs(rel)
    large = max_exact + (np.log(np.maximum(n, 1) / max_exact) / np.log(MAX_DISTANCE / max_exact)
                         * (half - max_exact)).astype(np.int32)
    large = np.minimum(large, half - 1)
    bucket = (rel > 0).astype(np.int32) * half + np.where(n < max_exact, n, large)
    return bucket.astype(np.int32)


def _rel_bias_kernel(table_ref, bucket_ref, o_ref):
    bucket = bucket_ref[...]
    for h in range(N_Q_HEADS):
        bias = jnp.zeros(bucket.shape, F32)
        for bkt in range(N_BUCKETS):
            bias = jnp.where(bucket == bkt, table_ref[bkt, h], bias)
        o_ref[h] = bias


def _rel_bias(table):
    return pl.pallas_call(
        _rel_bias_kernel,
        in_specs=[pl.BlockSpec(memory_space=pltpu.SMEM), pl.BlockSpec(memory_space=pltpu.VMEM)],
        out_specs=pl.BlockSpec(memory_space=pltpu.VMEM),
        out_shape=jax.ShapeDtypeStruct((N_Q_HEADS, BLOCK, 3 * BLOCK), F32),
        name="rel_bias",
    )(table, jnp.asarray(_bucket_table()))


def _prep_layer(l, ln1_g, ln1_b, ffn1_w_gate_up, ffn1_w_down, w_in, conv_w, w_mem_kv, attn_sink, w_gate,
                b_gate, w_branch, w_o, ln2_g, ln2_b, ffn2_w_gate_up, ffn2_w_down, ln3_g, ln3_b):
    pad = D_FF_PAD - D_FF

    def ffn(w_gate_up, w_down):
        wg = jnp.pad(w_gate_up[l, :, :D_FF].astype(BF16), ((0, 0), (0, pad)))
        wu = jnp.pad(w_gate_up[l, :, D_FF:].astype(BF16), ((0, 0), (0, pad)))
        wd = jnp.pad(w_down[l].astype(BF16), ((0, pad), (0, 0)))
        return wg, wu, wd

    vec = lambda a: a[l].reshape(1, -1).astype(F32)
    return dict(
        ffn1=ffn(ffn1_w_gate_up, ffn1_w_down), ln1=(vec(ln1_g), vec(ln1_b)),
        w_in=w_in[l].astype(BF16), conv_w=conv_w[l].astype(F32), w_mem_kv=w_mem_kv[l].astype(BF16),
        sink=attn_sink[l].astype(F32), w_gate=w_gate[l].astype(BF16), b_gate=vec(b_gate),
        w_branch=w_branch[l].astype(BF16), w_o=w_o[l].astype(BF16), ln2=(vec(ln2_g), vec(ln2_b)),
        ffn2=ffn(ffn2_w_gate_up, ffn2_w_down), ln3=(vec(ln3_g), vec(ln3_b)),
    )


def _layer(x, mem, rel_bias, p, bsz, seq):
    x1 = _ffn_ln(x, *p["ffn1"], *p["ln1"])
    qkv, cb, u, qx = _in_proj(x1, p["w_in"])
    gates = _matmul(x1, p["w_gate"], p["b_gate"], out_dtype=BF16)
    kvm = _matmul(mem, p["w_mem_kv"], out_dtype=BF16)
    ymix = _mixers(
        qkv.reshape(bsz, seq, D_QKV), cb.reshape(bsz, seq, D_CONV), u.reshape(bsz, seq, D_CONV),
        qx.reshape(bsz, seq, D_XATTN), kvm.reshape(bsz, N_MEM, 2 * D_XATTN), rel_bias, p["sink"], p["conv_w"])
    x2 = _merge_ln(x1, ymix.reshape(bsz * seq, D_MIX), gates, p["w_branch"], p["w_o"], *p["ln2"])
    return _ffn_ln(x2, *p["ffn2"], *p["ln3"])


def _forward(x, mem, rel_bias, layers):
    bsz, seq, _ = x.shape
    h = x.reshape(bsz * seq, D_MODEL)
    mem2 = mem.reshape(bsz * N_MEM, D_MODEL)
    for p in layers:
        h = _layer(h, mem2, rel_bias, p, bsz, seq)
    return h.reshape(bsz, seq, D_MODEL)


def kernel(x_prompt, x_sample, mem_prompt, mem_sample, rel_bias_table, ln1_g, ln1_b, ffn1_w_gate_up, ffn1_w_down, w_in, conv_w, w_mem_kv, attn_sink, w_gate, b_gate, w_branch, w_o, ln2_g, ln2_b, ffn2_w_gate_up, ffn2_w_down, ln3_g, ln3_b):
    weights = (ln1_g, ln1_b, ffn1_w_gate_up, ffn1_w_down, w_in, conv_w, w_mem_kv, attn_sink, w_gate, b_gate,
               w_branch, w_o, ln2_g, ln2_b, ffn2_w_gate_up, ffn2_w_down, ln3_g, ln3_b)
    layers = [_prep_layer(l, *weights) for l in range(ln1_g.shape[0])]
    rel_bias = _rel_bias(rel_bias_table.astype(F32))
    y_prompt = _forward(x_prompt, mem_prompt, rel_bias, layers)
    y_sample = _forward(x_sample, mem_sample, rel_bias, layers)
    return (y_prompt, y_sample)
```

```python
import functools

import jax
import jax.numpy as jnp
import numpy as np
from jax import lax
from jax.experimental import pallas as pl
from jax.experimental.pallas import tpu as pltpu

D_MODEL = 2048
DEPTH = 1
HEAD_DIM = 128
N_Q_HEADS = 8
N_KV_HEADS = 2
GQA_GROUP = N_Q_HEADS // N_KV_HEADS
D_ATTN = N_Q_HEADS * HEAD_DIM
D_KV = N_KV_HEADS * HEAD_DIM
WINDOW = 128
BLOCK = 128
D_CONV = D_MODEL // 4
CONV_WIDTH = 3
N_MEM = 256
N_X_HEADS = 4
D_XATTN = N_X_HEADS * HEAD_DIM
D_MIX = D_ATTN + D_CONV + D_XATTN
D_QKV = D_ATTN + 2 * D_KV
D_IN = D_QKV + 3 * D_CONV + D_XATTN
N_BRANCH = 3
D_FF = 5504
N_BUCKETS = 32
MAX_DISTANCE = 128
LN_EPS = 1e-5
NEG_INF = -1e30
ALPHA = (2 * DEPTH) ** 0.25
ATTN_SCALE = HEAD_DIM ** -0.5

VMEM_LIMIT_BYTES = 58 * 1024 * 1024
SUBLANES = 8
FF_TILE = 1024
FF_STEPS = -(-D_FF // FF_TILE)
FF_LAST = D_FF - (FF_STEPS - 1) * FF_TILE

F32 = jnp.float32
BF16 = jnp.bfloat16


def _pick_tile(n, target):
    t = min(n, target)
    while n % t:
        t //= 2
    return t


def _params(*semantics):
    return pltpu.CompilerParams(dimension_semantics=semantics, vmem_limit_bytes=VMEM_LIMIT_BYTES)


def _resident(shape):
    zeros = (0,) * len(shape)
    return pl.BlockSpec(shape, lambda *_: zeros, pipeline_mode=pl.Buffered(1))


def _layer_norm(y, g, b):
    mu = jnp.mean(y, axis=-1, keepdims=True)
    yc = y - mu
    var = jnp.mean(yc * yc, axis=-1, keepdims=True)
    return yc * lax.rsqrt(var + LN_EPS) * g + b


def _ffn_ln_kernel(x_hbm, wg_ref, wu_ref, wd_ref, g_ref, b_ref, o_hbm, xbuf, xb_ref, acc_ref, x_sem, o_sem,
                   *, tm, rc):
    i = pl.program_id(0)
    k = pl.program_id(1)
    n_i = pl.num_programs(0)
    n_k = pl.num_programs(1)
    chunks = [pl.ds(c * rc, rc) for c in range(tm // rc)]

    def x_copy(tile):
        return pltpu.make_async_copy(x_hbm.at[pl.ds(tile * tm, tm), :], xbuf, x_sem)

    def o_copy(tile):
        return pltpu.make_async_copy(acc_ref, o_hbm.at[pl.ds(tile * tm, tm), :], o_sem)

    def swiglu(xb, width=FF_TILE):
        hg = jnp.dot(xb, wg_ref[:, :width], preferred_element_type=F32)
        hu = jnp.dot(xb, wu_ref[:, :width], preferred_element_type=F32)
        return (hg * jax.nn.sigmoid(hg) * hu).astype(BF16)

    def down(act):
        return jnp.dot(act, wd_ref[:act.shape[1], :], preferred_element_type=F32)

    @pl.when((i == 0) & (k == 0))
    def _():
        x_copy(0).start()

    @pl.when(k == 0)
    def _():
        x_copy(i).wait()
        for rows in chunks:
            xc = xbuf[rows, :]
            xbc = xc.astype(BF16)
            xb_ref[rows, :] = xbc
            xbuf[rows, :] = (2.0 * ALPHA) * xc + down(swiglu(xbc))

    @pl.when((k == 1) & (i > 0))
    def _():
        o_copy(i - 1).wait()

    @pl.when(k == 1)
    def _():
        acc_ref[...] = xbuf[...] + down(swiglu(xb_ref[...]))

    @pl.when((k == 2) & (i + 1 < n_i))
    def _():
        x_copy(i + 1).start()

    @pl.when((k > 1) & (k < n_k - 1))
    def _():
        acc_ref[...] += down(swiglu(xb_ref[...]))

    @pl.when(k == n_k - 1)
    def _():
        act = swiglu(xb_ref[...], FF_LAST)
        for c, rows in enumerate(chunks):
            acc = acc_ref[rows, :] + down(act[c * rc:(c + 1) * rc])
            acc_ref[rows, :] = _layer_norm(0.5 * acc, g_ref[...], b_ref[...])
        o_copy(i).start()

    @pl.when((k == n_k - 1) & (i == n_i - 1))
    def _():
        o_copy(i).wait()


def _ffn_ln(x, wg, wu, wd, g, b, *, tm_target=1024, rc_target=256):
    m = x.shape[0]
    tm = _pick_tile(m, tm_target)
    rc = _pick_tile(tm, rc_target)
    assert FF_STEPS >= 4
    vec = pl.BlockSpec((1, D_MODEL), lambda i, k: (0, 0))
    return pl.pallas_call(
        functools.partial(_ffn_ln_kernel, tm=tm, rc=rc),
        grid=(m // tm, FF_STEPS),
        in_specs=[
            pl.BlockSpec(memory_space=pl.ANY),
            pl.BlockSpec((D_MODEL, FF_TILE), lambda i, k: (0, k)),
            pl.BlockSpec((D_MODEL, FF_TILE), lambda i, k: (0, k)),
            pl.BlockSpec((FF_TILE, D_MODEL), lambda i, k: (k, 0)),
            vec,
            vec,
        ],
        out_specs=pl.BlockSpec(memory_space=pl.ANY),
        out_shape=jax.ShapeDtypeStruct((m, D_MODEL), F32),
        scratch_shapes=[
            pltpu.VMEM((tm, D_MODEL), F32),
            pltpu.VMEM((tm, D_MODEL), BF16),
            pltpu.VMEM((tm, D_MODEL), F32),
            pltpu.SemaphoreType.DMA(()),
            pltpu.SemaphoreType.DMA(()),
        ],
        compiler_params=_params("arbitrary", "arbitrary"),
        name="ffn_ln",
    )(x, wg, wu, wd, g, b)


def _matmul_kernel(x_ref, w_ref, *refs, sigmoid):
    if sigmoid:
        b_ref, o_ref = refs
    else:
        (o_ref,) = refs
    acc = jnp.dot(x_ref[...].astype(BF16), w_ref[...], preferred_element_type=F32)
    if sigmoid:
        acc = jax.nn.sigmoid(acc + b_ref[...])
    o_ref[...] = acc.astype(o_ref.dtype)


def _matmul(x, w, bias=None, *, out_dtype, tm_target=1024, tn_target=1024):
    m, kdim = x.shape
    n = w.shape[1]
    tm = _pick_tile(m, tm_target)
    tn = _pick_tile(n, tn_target)
    in_specs = [
        pl.BlockSpec((tm, kdim), lambda i, j: (i, 0)),
        pl.BlockSpec((kdim, tn), lambda i, j: (0, j)),
    ]
    args = [x, w]
    if bias is not None:
        in_specs.append(pl.BlockSpec((1, tn), lambda i, j: (0, j)))
        args.append(bias)
    return pl.pallas_call(
        functools.partial(_matmul_kernel, sigmoid=bias is not None),
        grid=(m // tm, n // tn),
        in_specs=in_specs,
        out_specs=pl.BlockSpec((tm, tn), lambda i, j: (i, j)),
        out_shape=jax.ShapeDtypeStruct((m, n), out_dtype),
        compiler_params=_params("parallel", "arbitrary"),
        name="matmul_sigmoid" if bias is not None else "matmul",
    )(*args)


def _in_proj_kernel(x_ref, w_ref, qkv_ref, cb_ref, u_ref, qx_ref):
    x = x_ref[...].astype(BF16)

    def proj(lo, width):
        return jnp.dot(x, w_ref[:, lo:lo + width], preferred_element_type=F32)

    qkv_ref[...] = proj(0, D_QKV).astype(BF16)
    cb_ref[...] = proj(D_QKV, D_CONV)
    u_ref[...] = proj(D_QKV + D_CONV, D_CONV) * proj(D_QKV + 2 * D_CONV, D_CONV)
    qx_ref[...] = proj(D_QKV + 3 * D_CONV, D_XATTN).astype(BF16)


def _in_proj(x, w_in, *, tm_target=1024):
    m = x.shape[0]
    tm = _pick_tile(m, tm_target)

    def row(width):
        return pl.BlockSpec((tm, width), lambda i: (i, 0))

    return pl.pallas_call(
        _in_proj_kernel,
        grid=(m // tm,),
        in_specs=[row(D_MODEL), _resident((D_MODEL, D_IN))],
        out_specs=[row(D_QKV), row(D_CONV), row(D_CONV), row(D_XATTN)],
        out_shape=[
            jax.ShapeDtypeStruct((m, D_QKV), BF16),
            jax.ShapeDtypeStruct((m, D_CONV), F32),
            jax.ShapeDtypeStruct((m, D_CONV), F32),
            jax.ShapeDtypeStruct((m, D_XATTN), BF16),
        ],
        compiler_params=_params("parallel"),
        name="in_proj",
    )(x, w_in)


def _dot_nt(a, b):
    return lax.dot_general(a, b, (((1,), (1,)), ((), ())), preferred_element_type=F32)


def _mixers_kernel(sink_ref, q_ref, kp_ref, kc_ref, kn_ref, vp_ref, vc_ref, vn_ref, bias_ref,
                   cb_ref, up_ref, uc_ref, un_ref, cw_ref, qx_ref, kvm_ref, o_ref, *, tq, n_blocks):
    i = pl.program_id(1)
    last_i = pl.num_programs(1) - 1
    r_blocks = tq // BLOCK

    col = lax.broadcasted_iota(jnp.int32, (BLOCK, 3 * BLOCK), 1)
    rowi = lax.broadcasted_iota(jnp.int32, (BLOCK, 3 * BLOCK), 0)
    band = jnp.abs(col - BLOCK - rowi) <= WINDOW
    for r in range(r_blocks):
        blk = i * r_blocks + r
        lo = jnp.where(blk == 0, BLOCK, 0)
        hi = jnp.where(blk == n_blocks - 1, 2 * BLOCK, 3 * BLOCK)
        valid = band & (col >= lo) & (col < hi)

        def window(p_ref, c_ref, n_ref):
            parts = []
            parts.append(p_ref[0] if r == 0 else c_ref[0, (r - 1) * BLOCK:r * BLOCK, :])
            parts.append(c_ref[0, r * BLOCK:(r + 1) * BLOCK, :])
            parts.append(n_ref[0] if r == r_blocks - 1 else c_ref[0, (r + 1) * BLOCK:(r + 2) * BLOCK, :])
            return jnp.concatenate(parts, axis=0)

        k3 = window(kp_ref, kc_ref, kn_ref)
        v3 = window(vp_ref, vc_ref, vn_ref)
        rows = slice(r * BLOCK, (r + 1) * BLOCK)
        for g in range(N_KV_HEADS):
            kg = k3[:, g * HEAD_DIM:(g + 1) * HEAD_DIM]
            vg = v3[:, g * HEAD_DIM:(g + 1) * HEAD_DIM]
            heads = [g * GQA_GROUP + j for j in range(GQA_GROUP)]
            q4 = jnp.concatenate(
                [q_ref[0, rows, h * HEAD_DIM:(h + 1) * HEAD_DIM] for h in heads], axis=0)
            s4 = _dot_nt(q4, kg) * ATTN_SCALE
            ps, denoms = [], []
            for j, h in enumerate(heads):
                s = s4[j * BLOCK:(j + 1) * BLOCK] + bias_ref[h]
                s = jnp.where(valid, s, NEG_INF)
                sink = sink_ref[h]
                m = jnp.maximum(jnp.max(s, axis=-1, keepdims=True), sink)
                p = jnp.exp(s - m)
                denoms.append(jnp.sum(p, axis=-1, keepdims=True) + jnp.exp(sink - m))
                ps.append(p.astype(BF16))
            o4 = jnp.dot(jnp.concatenate(ps, axis=0), vg, preferred_element_type=F32)
            for j, h in enumerate(heads):
                o = o4[j * BLOCK:(j + 1) * BLOCK] / denoms[j]
                o_ref[0, rows, h * HEAD_DIM:(h + 1) * HEAD_DIM] = o.astype(BF16)

    u = uc_ref[0]
    t = lax.broadcasted_iota(jnp.int32, (tq, D_CONV), 0)
    prev_row = jnp.where(i == 0, 0.0, up_ref[0, SUBLANES - 1:SUBLANES, :])
    next_row = jnp.where(i == last_i, 0.0, un_ref[0, 0:1, :])
    u_prev = jnp.where(t == 0, prev_row, pltpu.roll(u, 1, axis=0))
    u_next = jnp.where(t == tq - 1, next_row, pltpu.roll(u, tq - 1, axis=0))
    conv = u_prev * cw_ref[0:1, :] + u * cw_ref[1:2, :] + u_next * cw_ref[2:3, :]
    o_ref[0, :, D_ATTN:D_ATTN + D_CONV] = (cb_ref[0] * conv).astype(BF16)

    for h in range(N_X_HEADS):
        cols = slice(h * HEAD_DIM, (h + 1) * HEAD_DIM)
        km = kvm_ref[0, :, cols]
        vm = kvm_ref[0, :, D_XATTN + h * HEAD_DIM:D_XATTN + (h + 1) * HEAD_DIM]
        s = _dot_nt(qx_ref[0, :, cols], km) * ATTN_SCALE
        p = jnp.exp(s - jnp.max(s, axis=-1, keepdims=True))
        denom = jnp.sum(p, axis=-1, keepdims=True)
        o = jnp.dot(p.astype(BF16), vm, preferred_element_type=F32) / denom
        base = D_ATTN + D_CONV
        o_ref[0, :, base + h * HEAD_DIM:base + (h + 1) * HEAD_DIM] = o.astype(BF16)


def _mixers(qkv, cb, u, qx, kvm, rel_bias, sink, conv_w, *, tq_target=512):
    bsz, seq, _ = qkv.shape
    tq = _pick_tile(seq, tq_target)
    r_blocks = tq // BLOCK
    n_blocks = seq // BLOCK
    n_oct = seq // SUBLANES
    oct_per_tile = tq // SUBLANES
    k_col = D_ATTN // D_KV
    v_col = k_col + 1

    def main(width, colblk=0):
        return pl.BlockSpec((1, tq, width), lambda b, i: (b, i, colblk))

    def prev_blk(colblk):
        return pl.BlockSpec((1, BLOCK, D_KV), lambda b, i: (b, jnp.maximum(i * r_blocks - 1, 0), colblk))

    def next_blk(colblk):
        return pl.BlockSpec(
            (1, BLOCK, D_KV), lambda b, i: (b, jnp.minimum((i + 1) * r_blocks, n_blocks - 1), colblk))

    in_specs = [
        pl.BlockSpec(memory_space=pltpu.SMEM),
        main(D_ATTN),
        prev_blk(k_col), main(D_KV, k_col), next_blk(k_col),
        prev_blk(v_col), main(D_KV, v_col), next_blk(v_col),
        pl.BlockSpec((N_Q_HEADS, BLOCK, 3 * BLOCK), lambda b, i: (0, 0, 0)),
        main(D_CONV),
        pl.BlockSpec((1, SUBLANES, D_CONV), lambda b, i: (b, jnp.maximum(i * oct_per_tile - 1, 0), 0)),
        main(D_CONV),
        pl.BlockSpec((1, SUBLANES, D_CONV), lambda b, i: (b, jnp.minimum((i + 1) * oct_per_tile, n_oct - 1), 0)),
        pl.BlockSpec((CONV_WIDTH, D_CONV), lambda b, i: (0, 0)),
        main(D_XATTN),
        pl.BlockSpec((1, N_MEM, 2 * D_XATTN), lambda b, i: (b, 0, 0)),
    ]
    return pl.pallas_call(
        functools.partial(_mixers_kernel, tq=tq, n_blocks=n_blocks),
        grid=(bsz, seq // tq),
        in_specs=in_specs,
        out_specs=pl.BlockSpec((1, tq, D_MIX), lambda b, i: (b, i, 0)),
        out_shape=jax.ShapeDtypeStruct((bsz, seq, D_MIX), BF16),
        compiler_params=_params("parallel", "arbitrary"),
        name="mixers",
    )(sink, qkv, qkv, qkv, qkv, qkv, qkv, qkv, rel_bias, cb, u, u, u, conv_w, qx, kvm)


MERGE_COLS = 512


def _merge_ln_kernel(x_ref, y_ref, wg_ref, bg_ref, wb_ref, wo_ref, g_ref, b_ref, o_ref, *, rc):
    bounds = (0, D_ATTN, D_ATTN + D_CONV, D_MIX)
    tm = x_ref.shape[0]
    xb = x_ref[...].astype(BF16)
    parts = []
    for c0 in range(0, D_MODEL, MERGE_COLS):
        merged = None
        for br in range(N_BRANCH):
            lo, hi = bounds[br], bounds[br + 1]
            gcols = slice(br * D_MODEL + c0, br * D_MODEL + c0 + MERGE_COLS)
            gate = jax.nn.sigmoid(
                jnp.dot(xb, wg_ref[:, gcols], preferred_element_type=F32) + bg_ref[:, gcols])
            proj = jnp.dot(y_ref[:, lo:hi], wb_ref[lo:hi, c0:c0 + MERGE_COLS], preferred_element_type=F32)
            term = gate * proj
            merged = term if merged is None else merged + term
        parts.append(merged.astype(BF16))
    merged = jnp.concatenate(parts, axis=1)
    for r0 in range(0, tm, rc):
        rows = pl.ds(r0, rc)
        mixed = jnp.dot(merged[r0:r0 + rc], wo_ref[...], preferred_element_type=F32)
        o_ref[rows, :] = _layer_norm(ALPHA * x_ref[rows, :] + mixed, g_ref[...], b_ref[...])


def _merge_ln(x1, ymix, w_gate, b_gate, wb, wo, g, b, *, tm_target=256, rc_target=128):
    m = x1.shape[0]
    tm = _pick_tile(m, tm_target)
    rc = _pick_tile(tm, rc_target)

    def row(width):
        return pl.BlockSpec((tm, width), lambda i: (i, 0))

    return pl.pallas_call(
        functools.partial(_merge_ln_kernel, rc=rc),
        grid=(m // tm,),
        in_specs=[
            row(D_MODEL), row(D_MIX),
            _resident((D_MODEL, N_BRANCH * D_MODEL)), _resident((1, N_BRANCH * D_MODEL)),
            _resident((D_MIX, D_MODEL)), _resident((D_MODEL, D_MODEL)),
            _resident((1, D_MODEL)), _resident((1, D_MODEL)),
        ],
        out_specs=row(D_MODEL),
        out_shape=jax.ShapeDtypeStruct((m, D_MODEL), F32),
        compiler_params=_params("parallel"),
        name="merge_ln",
    )(x1, ymix, w_gate, b_gate, wb, wo, g, b)


def _bucket_table():
    t = np.arange(BLOCK)[:, None]
    j = np.arange(3 * BLOCK)[None, :]
    rel = j - BLOCK - t
    half = N_BUCKETS // 2
    max_exact = half // 2
    n = np.abs(rel)
    large = max_exact + (np.log(np.maximum(n, 1) / max_exact) / np.log(MAX_DISTANCE / max_exact)
                         * (half - max_exact)).astype(np.int32)
    large = np.minimum(large, half - 1)
    bucket = (rel > 0).astype(np.int32) * half + np.where(n < max_exact, n, large)
    return bucket.astype(np.int32)


def _rel_bias_kernel(table_ref, bucket_ref, o_ref):
    bucket = bucket_ref[...]
    for h in range(N_Q_HEADS):
        bias = jnp.zeros(bucket.shape, F32)
        for bkt in range(N_BUCKETS):
            bias = jnp.where(bucket == bkt, table_ref[bkt, h], bias)
        o_ref[h] = bias


def _rel_bias(table):
    return pl.pallas_call(
        _rel_bias_kernel,
        in_specs=[pl.BlockSpec(memory_space=pltpu.SMEM), pl.BlockSpec(memory_space=pltpu.VMEM)],
        out_specs=pl.BlockSpec(memory_space=pltpu.VMEM),
        out_shape=jax.ShapeDtypeStruct((N_Q_HEADS, BLOCK, 3 * BLOCK), F32),
        name="rel_bias",
    )(table, jnp.asarray(_bucket_table()))


def _prep_layer(l, ln1_g, ln1_b, ffn1_w_gate_up, ffn1_w_down, w_in, conv_w, w_mem_kv, attn_sink, w_gate,
                b_gate, w_branch, w_o, ln2_g, ln2_b, ffn2_w_gate_up, ffn2_w_down, ln3_g, ln3_b):
    def ffn(w_gate_up, w_down):
        return w_gate_up[l, :, :D_FF].astype(BF16), w_gate_up[l, :, D_FF:].astype(BF16), w_down[l].astype(BF16)

    vec = lambda a: a[l].reshape(1, -1).astype(F32)
    return dict(
        ffn1=ffn(ffn1_w_gate_up, ffn1_w_down), ln1=(vec(ln1_g), vec(ln1_b)),
        w_in=w_in[l].astype(BF16), conv_w=conv_w[l].astype(F32), w_mem_kv=w_mem_kv[l].astype(BF16),
        sink=attn_sink[l].astype(F32), w_gate=w_gate[l].astype(BF16), b_gate=vec(b_gate),
        w_branch=w_branch[l].astype(BF16), w_o=w_o[l].astype(BF16), ln2=(vec(ln2_g), vec(ln2_b)),
        ffn2=ffn(ffn2_w_gate_up, ffn2_w_down), ln3=(vec(ln3_g), vec(ln3_b)),
    )


def _layer(x, mem, rel_bias, p, bsz, seq):
    x1 = _ffn_ln(x, *p["ffn1"], *p["ln1"])
    qkv, cb, u, qx = _in_proj(x1, p["w_in"])
    kvm = _matmul(mem, p["w_mem_kv"], out_dtype=BF16)
    ymix = _mixers(
        qkv.reshape(bsz, seq, D_QKV), cb.reshape(bsz, seq, D_CONV), u.reshape(bsz, seq, D_CONV),
        qx.reshape(bsz, seq, D_XATTN), kvm.reshape(bsz, N_MEM, 2 * D_XATTN), rel_bias, p["sink"], p["conv_w"])
    x2 = _merge_ln(x1, ymix.reshape(bsz * seq, D_MIX), p["w_gate"], p["b_gate"], p["w_branch"], p["w_o"], *p["ln2"])
    return _ffn_ln(x2, *p["ffn2"], *p["ln3"])


def _forward(x, mem, rel_bias, layers):
    bsz, seq, _ = x.shape
    h = x.reshape(bsz * seq, D_MODEL)
    mem2 = mem.reshape(bsz * N_MEM, D_MODEL)
    for p in layers:
        h = _layer(h, mem2, rel_bias, p, bsz, seq)
    return h.reshape(bsz, seq, D_MODEL)


def kernel(x_prompt, x_sample, mem_prompt, mem_sample, rel_bias_table, ln1_g, ln1_b, ffn1_w_gate_up, ffn1_w_down, w_in, conv_w, w_mem_kv, attn_sink, w_gate, b_gate, w_branch, w_o, ln2_g, ln2_b, ffn2_w_gate_up, ffn2_w_down, ln3_g, ln3_b):
    weights = (ln1_g, ln1_b, ffn1_w_gate_up, ffn1_w_down, w_in, conv_w, w_mem_kv, attn_sink, w_gate, b_gate,
               w_branch, w_o, ln2_g, ln2_b, ffn2_w_gate_up, ffn2_w_down, ln3_g, ln3_b)
    layers = [_prep_layer(l, *weights) for l in range(ln1_g.shape[0])]
    rel_bias = _rel_bias(rel_bias_table.astype(F32))
    y_prompt = _forward(x_prompt, mem_prompt, rel_bias, layers)
    y_sample = _forward(x_sample, mem_sample, rel_bias, layers)
    return (y_prompt, y_sample)
```

```python
import functools

import jax
import jax.numpy as jnp
import numpy as np
from jax import lax
from jax.experimental import pallas as pl
from jax.experimental.pallas import tpu as pltpu

D_MODEL = 2048
DEPTH = 1
HEAD_DIM = 128
N_Q_HEADS = 8
N_KV_HEADS = 2
GQA_GROUP = N_Q_HEADS // N_KV_HEADS
D_ATTN = N_Q_HEADS * HEAD_DIM
D_KV = N_KV_HEADS * HEAD_DIM
WINDOW = 128
BLOCK = 128
D_CONV = D_MODEL // 4
CONV_WIDTH = 3
N_MEM = 256
N_X_HEADS = 4
D_XATTN = N_X_HEADS * HEAD_DIM
D_MIX = D_ATTN + D_CONV + D_XATTN
D_QKV = D_ATTN + 2 * D_KV
D_IN = D_QKV + 3 * D_CONV + D_XATTN
N_BRANCH = 3
D_FF = 5504
N_BUCKETS = 32
MAX_DISTANCE = 128
LN_EPS = 1e-5
NEG_INF = -1e30
ALPHA = (2 * DEPTH) ** 0.25
ATTN_SCALE = HEAD_DIM ** -0.5
LOG2E = 1.4426950408889634
ATTN_SCALE_LOG2 = ATTN_SCALE * LOG2E

VMEM_LIMIT_BYTES = 58 * 1024 * 1024
SUBLANES = 8
LANES = 128
FF_TILE = 1024
FF_STEPS = -(-D_FF // FF_TILE)
FF_LAST = D_FF - (FF_STEPS - 1) * FF_TILE
MXU_COLS = 256
FF_LAST_MAIN = FF_LAST // MXU_COLS * MXU_COLS
FF_TAIL = FF_LAST - FF_LAST_MAIN
assert 2 * FF_TAIL == MXU_COLS
FF_U_SHIFT = FF_TILE - FF_LAST

F32 = jnp.float32
BF16 = jnp.bfloat16


def _pick_tile(n, target):
    t = min(n, target)
    while n % t:
        t //= 2
    return t


def _params(*semantics):
    return pltpu.CompilerParams(dimension_semantics=semantics, vmem_limit_bytes=VMEM_LIMIT_BYTES)


def _resident(shape):
    zeros = (0,) * len(shape)
    return pl.BlockSpec(shape, lambda *_: zeros, pipeline_mode=pl.Buffered(1))


def _layer_norm(y, g, b):
    mu = jnp.mean(y, axis=-1, keepdims=True)
    yc = y - mu
    var = jnp.mean(yc * yc, axis=-1, keepdims=True)
    return yc * lax.rsqrt(var + LN_EPS) * g + b


def _ffn_ln_kernel(x_hbm, wg_ref, wu_ref, wd_ref, wt_ref, g_ref, b_ref, o_hbm, xbuf, xb_ref, acc_ref, x_sem,
                   o_sem, *, tm, rc):
    i = pl.program_id(0)
    k = pl.program_id(1)
    n_i = pl.num_programs(0)
    n_k = pl.num_programs(1)
    chunks = [pl.ds(c * rc, rc) for c in range(tm // rc)]

    def x_copy(tile):
        return pltpu.make_async_copy(x_hbm.at[pl.ds(tile * tm, tm), :], xbuf, x_sem)

    def o_copy(tile):
        return pltpu.make_async_copy(acc_ref, o_hbm.at[pl.ds(tile * tm, tm), :], o_sem)

    def swiglu(xb, width=FF_TILE):
        hg = jnp.dot(xb, wg_ref[:, :width], preferred_element_type=F32)
        hu = jnp.dot(xb, wu_ref[:, :width], preferred_element_type=F32)
        return (hg * jax.nn.sigmoid(hg) * hu).astype(BF16)

    def down(act):
        return jnp.dot(act, wd_ref[:act.shape[1], :], preferred_element_type=F32)

    @pl.when((i == 0) & (k == 0))
    def _():
        x_copy(0).start()

    @pl.when(k == 0)
    def _():
        x_copy(i).wait()
        for rows in chunks:
            xc = xbuf[rows, :]
            xbc = xc.astype(BF16)
            xb_ref[rows, :] = xbc
            xbuf[rows, :] = (2.0 * ALPHA) * xc + down(swiglu(xbc))

    @pl.when((k == 1) & (i > 0))
    def _():
        o_copy(i - 1).wait()

    @pl.when(k == 1)
    def _():
        acc_ref[...] = xbuf[...] + down(swiglu(xb_ref[...]))

    @pl.when((k == 2) & (i + 1 < n_i))
    def _():
        x_copy(i + 1).start()

    @pl.when((k > 1) & (k < n_k - 1))
    def _():
        acc_ref[...] += down(swiglu(xb_ref[...]))

    @pl.when(k == n_k - 1)
    def _():
        for rows in chunks:
            xb = xb_ref[rows, :]
            hg = jnp.dot(xb, wg_ref[:, :FF_LAST_MAIN], preferred_element_type=F32)
            hu = jnp.dot(xb, wu_ref[:, FF_U_SHIFT:FF_U_SHIFT + FF_LAST_MAIN], preferred_element_type=F32)
            ht = jnp.dot(xb, wt_ref[...], preferred_element_type=F32)
            hg = jnp.concatenate([hg, ht[:, :FF_TAIL]], axis=1)
            hu = jnp.concatenate([hu, ht[:, FF_TAIL:]], axis=1)
            act = (hg * jax.nn.sigmoid(hg) * hu).astype(BF16)
            acc = acc_ref[rows, :] + down(act)
            acc_ref[rows, :] = _layer_norm(0.5 * acc, g_ref[...], b_ref[...])
        o_copy(i).start()

    @pl.when((k == n_k - 1) & (i == n_i - 1))
    def _():
        o_copy(i).wait()


def _ffn_ln(x, wgu, wd, wt, g, b, *, tm_target=1024, rc_target=256):
    m = x.shape[0]
    tm = _pick_tile(m, tm_target)
    rc = _pick_tile(tm, rc_target)
    assert FF_STEPS >= 4
    vec = pl.BlockSpec((1, D_MODEL), lambda i, k: (0, 0))
    return pl.pallas_call(
        functools.partial(_ffn_ln_kernel, tm=tm, rc=rc),
        grid=(m // tm, FF_STEPS),
        in_specs=[
            pl.BlockSpec(memory_space=pl.ANY),
            pl.BlockSpec((D_MODEL, FF_TILE), lambda i, k: (0, k)),
            pl.BlockSpec((pl.Element(D_MODEL), pl.Element(FF_TILE)),
                         lambda i, k: (0, LANES * jnp.minimum(D_FF // LANES + k * (FF_TILE // LANES),
                                                              (2 * D_FF - FF_TILE) // LANES))),
            pl.BlockSpec((FF_TILE, D_MODEL), lambda i, k: (k, 0)),
            pl.BlockSpec((D_MODEL, MXU_COLS), lambda i, k: (0, 0)),
            vec,
            vec,
        ],
        out_specs=pl.BlockSpec(memory_space=pl.ANY),
        out_shape=jax.ShapeDtypeStruct((m, D_MODEL), F32),
        scratch_shapes=[
            pltpu.VMEM((tm, D_MODEL), F32),
            pltpu.VMEM((tm, D_MODEL), BF16),
            pltpu.VMEM((tm, D_MODEL), F32),
            pltpu.SemaphoreType.DMA(()),
            pltpu.SemaphoreType.DMA(()),
        ],
        compiler_params=_params("arbitrary", "arbitrary"),
        name="ffn_ln",
    )(x, wgu, wgu, wd, wt, g, b)


def _matmul_kernel(x_ref, w_ref, *refs, sigmoid):
    if sigmoid:
        b_ref, o_ref = refs
    else:
        (o_ref,) = refs
    acc = jnp.dot(x_ref[...].astype(BF16), w_ref[...], preferred_element_type=F32)
    if sigmoid:
        acc = jax.nn.sigmoid(acc + b_ref[...])
    o_ref[...] = acc.astype(o_ref.dtype)


def _matmul(x, w, bias=None, *, out_dtype, tm_target=1024, tn_target=1024):
    m, kdim = x.shape
    n = w.shape[1]
    tm = _pick_tile(m, tm_target)
    tn = _pick_tile(n, tn_target)
    in_specs = [
        pl.BlockSpec((tm, kdim), lambda i, j: (i, 0)),
        pl.BlockSpec((kdim, tn), lambda i, j: (0, j)),
    ]
    args = [x, w]
    if bias is not None:
        in_specs.append(pl.BlockSpec((1, tn), lambda i, j: (0, j)))
        args.append(bias)
    return pl.pallas_call(
        functools.partial(_matmul_kernel, sigmoid=bias is not None),
        grid=(m // tm, n // tn),
        in_specs=in_specs,
        out_specs=pl.BlockSpec((tm, tn), lambda i, j: (i, j)),
        out_shape=jax.ShapeDtypeStruct((m, n), out_dtype),
        compiler_params=_params("parallel", "arbitrary"),
        name="matmul_sigmoid" if bias is not None else "matmul",
    )(*args)


def _dot_nt(a, b):
    return lax.dot_general(a, b, (((1,), (1,)), ((), ())), preferred_element_type=F32)


def _proj_mix_kernel(sink_ref, x_ref, w_ref, bias_ref, cw_ref, kvm_ref, o_ref,
                     qkv_s, cb_s, u_s, qx_s, kv_carry, u_carry, *, tm, tiles_per_seq):
    i = pl.program_id(0)
    n_tiles = pl.num_programs(0) - 1
    cur = lax.rem(i, 2)
    prv = 1 - cur

    def project():
        x = x_ref[...].astype(BF16)

        def proj(lo, width):
            return jnp.dot(x, w_ref[:, lo:lo + width], preferred_element_type=F32)

        qkv_s[cur] = proj(0, D_QKV).astype(BF16)
        cb_s[cur] = proj(D_QKV, D_CONV)
        u_s[cur] = proj(D_QKV + D_CONV, D_CONV) * proj(D_QKV + 2 * D_CONV, D_CONV)
        qx_s[cur] = proj(D_QKV + 3 * D_CONV, D_XATTN).astype(BF16)

    def save_carry():
        kv_carry[...] = qkv_s[cur, tm - BLOCK:tm, D_ATTN:D_QKV]
        u_carry[...] = u_s[cur, tm - SUBLANES:tm, :]

    def mix():
        pos = lax.rem(i - 1, tiles_per_seq)
        _mix_tile(o_ref, sink_ref, bias_ref, cw_ref, kvm_ref.at[0], qkv_s.at[prv], cb_s.at[prv], u_s.at[prv],
                  qx_s.at[prv], kv_carry, qkv_s.at[cur], u_carry, u_s.at[cur],
                  first=pos == 0, last=pos == tiles_per_seq - 1, tq=tm)

    @pl.when(i == 0)
    def _():
        qkv_s[1] = jnp.zeros(qkv_s.shape[1:], BF16)
        u_s[1] = jnp.zeros(u_s.shape[1:], F32)
        project()

    @pl.when((i > 0) & (i < n_tiles))
    def _():
        save_carry()
        project()
        mix()

    @pl.when(i == n_tiles)
    def _():
        save_carry()
        mix()


def _proj_mix(x1, w_in, kvm, rel_bias, sink, conv_w, *, seq, tm_target=512):
    m = x1.shape[0]
    tm = _pick_tile(seq, tm_target)
    n_tiles = m // tm
    tiles_per_seq = seq // tm
    return pl.pallas_call(
        functools.partial(_proj_mix_kernel, tm=tm, tiles_per_seq=tiles_per_seq),
        grid=(n_tiles + 1,),
        in_specs=[
            pl.BlockSpec(memory_space=pltpu.SMEM),
            pl.BlockSpec((tm, D_MODEL), lambda i: (jnp.minimum(i, n_tiles - 1), 0)),
            _resident((D_MODEL, D_IN)),
            pl.BlockSpec((N_Q_HEADS, BLOCK, 3 * BLOCK), lambda i: (0, 0, 0)),
            pl.BlockSpec((CONV_WIDTH, D_CONV), lambda i: (0, 0)),
            pl.BlockSpec((1, N_MEM, 2 * D_XATTN), lambda i: (jnp.maximum(i - 1, 0) // tiles_per_seq, 0, 0)),
        ],
        out_specs=pl.BlockSpec((tm, D_MIX), lambda i: (jnp.maximum(i - 1, 0), 0)),
        out_shape=jax.ShapeDtypeStruct((m, D_MIX), BF16),
        scratch_shapes=[
            pltpu.VMEM((2, tm, D_QKV), BF16),
            pltpu.VMEM((2, tm, D_CONV), F32),
            pltpu.VMEM((2, tm, D_CONV), F32),
            pltpu.VMEM((2, tm, D_XATTN), BF16),
            pltpu.VMEM((BLOCK, 2 * D_KV), BF16),
            pltpu.VMEM((SUBLANES, D_CONV), F32),
        ],
        compiler_params=_params("arbitrary"),
        name="proj_mix",
    )(sink, x1, w_in, rel_bias, conv_w, kvm)


def _mix_tile(o_ref, sink_ref, bias_ref, cw_ref, kvm_ref, qkv_ref, cb_ref, u_ref, qx_ref, kv_prev_ref, next_ref,
              u_prev_ref, u_next_ref, *, first, last, tq):
    r_blocks = tq // BLOCK

    col = lax.broadcasted_iota(jnp.int32, (BLOCK, 3 * BLOCK), 1)
    rowi = lax.broadcasted_iota(jnp.int32, (BLOCK, 3 * BLOCK), 0)
    band = jnp.abs(col - BLOCK - rowi) <= WINDOW
    for r in range(r_blocks):
        lo = jnp.where(first, BLOCK, 0) if r == 0 else 0
        hi = jnp.where(last, 2 * BLOCK, 3 * BLOCK) if r == r_blocks - 1 else 3 * BLOCK
        valid = band & (col >= lo) & (col < hi)

        def kv_block(b):
            if b < 0:
                return kv_prev_ref[...]
            if b == r_blocks:
                return next_ref[0:BLOCK, D_ATTN:D_QKV]
            return qkv_ref[b * BLOCK:(b + 1) * BLOCK, D_ATTN:D_QKV]

        kv3 = jnp.concatenate([kv_block(r - 1), kv_block(r), kv_block(r + 1)], axis=0)
        rows = slice(r * BLOCK, (r + 1) * BLOCK)
        for g in range(N_KV_HEADS):
            kg = kv3[:, g * HEAD_DIM:(g + 1) * HEAD_DIM]
            vg = kv3[:, D_KV + g * HEAD_DIM:D_KV + (g + 1) * HEAD_DIM]
            heads = [g * GQA_GROUP + j for j in range(GQA_GROUP)]
            q4 = jnp.concatenate(
                [qkv_ref[rows, h * HEAD_DIM:(h + 1) * HEAD_DIM] for h in heads], axis=0)
            s4 = _dot_nt(q4, kg) * ATTN_SCALE_LOG2
            ps, denoms = [], []
            for j, h in enumerate(heads):
                s = s4[j * BLOCK:(j + 1) * BLOCK] + bias_ref[h]
                s = jnp.where(valid, s, NEG_INF)
                sink = sink_ref[h] * LOG2E
                m = jnp.maximum(jnp.max(s, axis=-1, keepdims=True), sink)
                p = jnp.exp2(s - m)
                denoms.append(jnp.sum(p, axis=-1, keepdims=True) + jnp.exp2(sink - m))
                ps.append(p.astype(BF16))
            o4 = jnp.dot(jnp.concatenate(ps, axis=0), vg, preferred_element_type=F32)
            for j, h in enumerate(heads):
                o = o4[j * BLOCK:(j + 1) * BLOCK] / denoms[j]
                o_ref[rows, h * HEAD_DIM:(h + 1) * HEAD_DIM] = o.astype(BF16)

    u = u_ref[...]
    t = lax.broadcasted_iota(jnp.int32, (tq, D_CONV), 0)
    prev_row = jnp.where(first, 0.0, u_prev_ref[SUBLANES - 1:SUBLANES, :])
    next_row = jnp.where(last, 0.0, u_next_ref[0:1, :])
    u_prev = jnp.where(t == 0, prev_row, pltpu.roll(u, 1, axis=0))
    u_next = jnp.where(t == tq - 1, next_row, pltpu.roll(u, tq - 1, axis=0))
    conv = u_prev * cw_ref[0:1, :] + u * cw_ref[1:2, :] + u_next * cw_ref[2:3, :]
    o_ref[:, D_ATTN:D_ATTN + D_CONV] = (cb_ref[...] * conv).astype(BF16)

    for h in range(N_X_HEADS):
        cols = slice(h * HEAD_DIM, (h + 1) * HEAD_DIM)
        km = kvm_ref[:, cols]
        vm = kvm_ref[:, D_XATTN + h * HEAD_DIM:D_XATTN + (h + 1) * HEAD_DIM]
        s = _dot_nt(qx_ref[:, cols], km) * ATTN_SCALE_LOG2
        p = jnp.exp2(s - jnp.max(s, axis=-1, keepdims=True))
        denom = jnp.sum(p, axis=-1, keepdims=True)
        o = jnp.dot(p.astype(BF16), vm, preferred_element_type=F32) / denom
        base = D_ATTN + D_CONV
        o_ref[:, base + h * HEAD_DIM:base + (h + 1) * HEAD_DIM] = o.astype(BF16)


MERGE_COLS = 512


def _merge_ln_kernel(x_ref, y_ref, wg_ref, bg_ref, wb_ref, wo_ref, g_ref, b_ref, o_ref, *, rc):
    bounds = (0, D_ATTN, D_ATTN + D_CONV, D_MIX)
    tm = x_ref.shape[0]
    xb = x_ref[...].astype(BF16)
    parts = []
    for c0 in range(0, D_MODEL, MERGE_COLS):
        merged = None
        for br in range(N_BRANCH):
            lo, hi = bounds[br], bounds[br + 1]
            gcols = slice(br * D_MODEL + c0, br * D_MODEL + c0 + MERGE_COLS)
            gate = jax.nn.sigmoid(
                jnp.dot(xb, wg_ref[:, gcols], preferred_element_type=F32) + bg_ref[:, gcols])
            proj = jnp.dot(y_ref[:, lo:hi], wb_ref[lo:hi, c0:c0 + MERGE_COLS], preferred_element_type=F32)
            term = gate * proj
            merged = term if merged is None else merged + term
        parts.append(merged.astype(BF16))
    merged = jnp.concatenate(parts, axis=1)
    for r0 in range(0, tm, rc):
        rows = pl.ds(r0, rc)
        mixed = jnp.dot(merged[r0:r0 + rc], wo_ref[...], preferred_element_type=F32)
        o_ref[rows, :] = _layer_norm(ALPHA * x_ref[rows, :] + mixed, g_ref[...], b_ref[...])


def _merge_ln(x1, ymix, w_gate, b_gate, wb, wo, g, b, *, tm_target=256, rc_target=128):
    m = x1.shape[0]
    tm = _pick_tile(m, tm_target)
    rc = _pick_tile(tm, rc_target)

    def row(width):
        return pl.BlockSpec((tm, width), lambda i: (i, 0))

    return pl.pallas_call(
        functools.partial(_merge_ln_kernel, rc=rc),
        grid=(m // tm,),
        in_specs=[
            row(D_MODEL), row(D_MIX),
            _resident((D_MODEL, N_BRANCH * D_MODEL)), _resident((1, N_BRANCH * D_MODEL)),
            _resident((D_MIX, D_MODEL)), _resident((D_MODEL, D_MODEL)),
            _resident((1, D_MODEL)), _resident((1, D_MODEL)),
        ],
        out_specs=row(D_MODEL),
        out_shape=jax.ShapeDtypeStruct((m, D_MODEL), F32),
        compiler_params=_params("parallel"),
        name="merge_ln",
    )(x1, ymix, w_gate, b_gate, wb, wo, g, b)


def _bucket_table():
    t = np.arange(BLOCK)[:, None]
    j = np.arange(3 * BLOCK)[None, :]
    rel = j - BLOCK - t
    half = N_BUCKETS // 2
    max_exact = half // 2
    n = np.abs(rel)
    large = max_exact + (np.log(np.maximum(n, 1) / max_exact) / np.log(MAX_DISTANCE / max_exact)
                         * (half - max_exact)).astype(np.int32)
    large = np.minimum(large, half - 1)
    bucket = (rel > 0).astype(np.int32) * half + np.where(n < max_exact, n, large)
    return bucket.astype(np.int32)


def _rel_bias_kernel(table_ref, bucket_ref, o_ref):
    bucket = bucket_ref[...]
    for h in range(N_Q_HEADS):
        bias = jnp.zeros(bucket.shape, F32)
        for bkt in range(N_BUCKETS):
            bias = jnp.where(bucket == bkt, table_ref[bkt, h], bias)
        o_ref[h] = bias * LOG2E


def _rel_bias(table):
    return pl.pallas_call(
        _rel_bias_kernel,
        in_specs=[pl.BlockSpec(memory_space=pltpu.SMEM), pl.BlockSpec(memory_space=pltpu.VMEM)],
        out_specs=pl.BlockSpec(memory_space=pltpu.VMEM),
        out_shape=jax.ShapeDtypeStruct((N_Q_HEADS, BLOCK, 3 * BLOCK), F32),
        name="rel_bias",
    )(table, jnp.asarray(_bucket_table()))


def _prep_layer(l, ln1_g, ln1_b, ffn1_w_gate_up, ffn1_w_down, w_in, conv_w, w_mem_kv, attn_sink, w_gate,
                b_gate, w_branch, w_o, ln2_g, ln2_b, ffn2_w_gate_up, ffn2_w_down, ln3_g, ln3_b):
    def ffn(w_gate_up, w_down):
        wgu = w_gate_up[l].astype(BF16)
        wt = jnp.concatenate([wgu[:, D_FF - FF_TAIL:D_FF], wgu[:, 2 * D_FF - FF_TAIL:]], axis=1)
        return wgu, w_down[l].astype(BF16), wt

    vec = lambda a: a[l].reshape(1, -1).astype(F32)
    return dict(
        ffn1=ffn(ffn1_w_gate_up, ffn1_w_down), ln1=(vec(ln1_g), vec(ln1_b)),
        w_in=w_in[l].astype(BF16), conv_w=conv_w[l].astype(F32), w_mem_kv=w_mem_kv[l].astype(BF16),
        sink=attn_sink[l].astype(F32), w_gate=w_gate[l].astype(BF16), b_gate=vec(b_gate),
        w_branch=w_branch[l].astype(BF16), w_o=w_o[l].astype(BF16), ln2=(vec(ln2_g), vec(ln2_b)),
        ffn2=ffn(ffn2_w_gate_up, ffn2_w_down), ln3=(vec(ln3_g), vec(ln3_b)),
    )


def _layer(x, mem, rel_bias, p, bsz, seq):
    x1 = _ffn_ln(x, *p["ffn1"], *p["ln1"])
    kvm = _matmul(mem, p["w_mem_kv"], out_dtype=BF16).reshape(bsz, N_MEM, 2 * D_XATTN)
    ymix = _proj_mix(x1, p["w_in"], kvm, rel_bias, p["sink"], p["conv_w"], seq=seq)
    x2 = _merge_ln(x1, ymix, p["w_gate"], p["b_gate"], p["w_branch"], p["w_o"], *p["ln2"])
    return _ffn_ln(x2, *p["ffn2"], *p["ln3"])


def _forward(x, mem, rel_bias, layers):
    bsz, seq, _ = x.shape
    h = x.reshape(bsz * seq, D_MODEL)
    mem2 = mem.reshape(bsz * N_MEM, D_MODEL)
    for p in layers:
        h = _layer(h, mem2, rel_bias, p, bsz, seq)
    return h.reshape(bsz, seq, D_MODEL)


def kernel(x_prompt, x_sample, mem_prompt, mem_sample, rel_bias_table, ln1_g, ln1_b, ffn1_w_gate_up, ffn1_w_down, w_in, conv_w, w_mem_kv, attn_sink, w_gate, b_gate, w_branch, w_o, ln2_g, ln2_b, ffn2_w_gate_up, ffn2_w_down, ln3_g, ln3_b):
    weights = (ln1_g, ln1_b, ffn1_w_gate_up, ffn1_w_down, w_in, conv_w, w_mem_kv, attn_sink, w_gate, b_gate,
               w_branch, w_o, ln2_g, ln2_b, ffn2_w_gate_up, ffn2_w_down, ln3_g, ln3_b)
    layers = [_prep_layer(l, *weights) for l in range(ln1_g.shape[0])]
    rel_bias = _rel_bias(rel_bias_table.astype(F32))
    y_prompt = _forward(x_prompt, mem_prompt, rel_bias, layers)
    y_sample = _forward(x_sample, mem_sample, rel_bias, layers)
    return (y_prompt, y_sample)
```
